```python
import math
import jax
import jax.numpy as jnp
from jax import lax
import numpy as np

D_MODEL = 1024
BATCH = 32
SEQ = 2048
DEPTH = 2
DEC_BATCH = 16
DEC_SEQ = 32
PAST_LEN = 1024

CHUNK = 64
N_LEFT_CHUNKS = 8
N_BAND = N_LEFT_CHUNKS + 1
BAND_ROWS = N_LEFT_CHUNKS * CHUNK
HEAD_DIM = 64
N_HEADS_A = 8
W_A = N_HEADS_A * HEAD_DIM
REL_CLIP = 128
N_HEADS_B = 8
W_B = N_HEADS_B * HEAD_DIM
W_C = 512
N_BLOCKS_C = 8
BLOCK_C = W_C // N_BLOCKS_C
CONV_W = 4
LRU_C = 8.0
PLE_DIM = 256
N_BRANCH = 3
ROPE_BASE = 10000.0
ALPHA = (2 * DEPTH) ** 0.25
BETA = (8 * DEPTH) ** -0.25
LN_EPS = 1e-5
NEG_INF = -1e30
SPLITS = [W_A] * 4 + [W_B] * 4 + [W_C] * 2 + [D_MODEL] * 3
N_IN = sum(SPLITS)

kernel_name = 'hybrid_chunk_stream_encoder_step'


def split_cols(u):
    out = []
    off = 0
    for w in SPLITS:
        out.append(u[..., off:off + w])
        off += w
    return out


def layer_norm(x, g, b):
    xf = x.astype(jnp.float32)
    mu = jnp.mean(xf, -1, keepdims=True)
    var = jnp.mean(jnp.square(xf - mu), -1, keepdims=True)
    y = (xf - mu) * lax.rsqrt(var + LN_EPS) * g.astype(jnp.float32) + b.astype(jnp.float32)
    return y.astype(x.dtype)


def group_norm_heads(y, g):
    b, t, h, d = y.shape
    yf = y.astype(jnp.float32)
    mu = jnp.mean(yf, -1, keepdims=True)
    var = jnp.mean(jnp.square(yf - mu), -1, keepdims=True)
    out = ((yf - mu) * lax.rsqrt(var + LN_EPS)).reshape(b, t, h * d) * g.astype(jnp.float32)
    return out.astype(y.dtype)


def rope(x, pos):
    half = HEAD_DIM // 2
    inv = ROPE_BASE ** (-jnp.arange(half, dtype=jnp.float32) / half)
    ang = pos.astype(jnp.float32)[:, None] * inv[None, :]
    c = jnp.cos(ang)[:, None, :]
    s = jnp.sin(ang)[:, None, :]
    xf = x.astype(jnp.float32)
    x1, x2 = xf[..., :half], xf[..., half:]
    return jnp.concatenate([x1 * c - x2 * s, x1 * s + x2 * c], -1).astype(x.dtype)


def rel_bias(table, qpos, kpos):
    idx = jnp.clip(qpos[:, None] - kpos[None, :], -REL_CLIP, REL_CLIP) + REL_CLIP
    return table.astype(jnp.float32)[:, idx]


def band_attention_prompt(q, k, v, table):
    b, s, h, d = q.shape
    nc = s // CHUNK
    qc = q.reshape(b, nc, CHUNK, h, d) * (d ** -0.5)
    pad = ((0, 0), (N_LEFT_CHUNKS, 0), (0, 0), (0, 0), (0, 0))
    kp = jnp.pad(k.reshape(b, nc, CHUNK, h, d), pad)
    vp = jnp.pad(v.reshape(b, nc, CHUNK, h, d), pad)
    scores = jnp.concatenate(
        [jnp.einsum('bcqhd,bckhd->bhcqk', qc, kp[:, o:o + nc]) for o in range(N_BAND)],
        axis=-1).astype(jnp.float32)
    qpos = N_LEFT_CHUNKS * CHUNK + jnp.arange(CHUNK)
    kpos = jnp.arange(N_BAND * CHUNK)
    scores = scores + rel_bias(table, qpos, kpos)[None, :, None]
    valid = (jnp.arange(nc)[:, None] + jnp.arange(N_BAND)[None, :]) >= N_LEFT_CHUNKS
    valid = jnp.repeat(valid, CHUNK, axis=1)
    scores = jnp.where(valid[None, None, :, None, :], scores, NEG_INF)
    probs = jax.nn.softmax(scores, axis=-1).astype(v.dtype)
    out = sum(jnp.einsum('bhcqk,bckhd->bcqhd', probs[..., o * CHUNK:(o + 1) * CHUNK], vp[:, o:o + nc])
              for o in range(N_BAND))
    return out.reshape(b, s, h, d)


def band_attention_sample(q, k, v, k_cache, v_cache, table):
    t = q.shape[1]
    c = k_cache.shape[1]
    d = q.shape[-1]
    keys = jnp.concatenate([k_cache.astype(k.dtype), k], axis=1)
    vals = jnp.concatenate([v_cache.astype(v.dtype), v], axis=1)
    qpos = PAST_LEN + jnp.arange(t)
    kpos = PAST_LEN - c + jnp.arange(c + t)
    scores = jnp.einsum('bqhd,bkhd->bhqk', q * (d ** -0.5), keys).astype(jnp.float32)
    scores = scores + rel_bias(table, qpos, kpos)[None]
    probs = jax.nn.softmax(scores, axis=-1).astype(v.dtype)
    return jnp.einsum('bhqk,bkhd->bqhd', probs, vals)


def retention(q, k, v, s0, blk):
    dt = q.dtype
    b, t, h, d = q.shape
    n = t // blk
    q = q.astype(jnp.float32).reshape(b, n, blk, h, d)
    k = k.astype(jnp.float32).reshape(b, n, blk, h, d)
    v = v.astype(jnp.float32).reshape(b, n, blk, h, d)
    log_g = jnp.log1p(-jnp.exp2(-5.0 - jnp.arange(h, dtype=jnp.float32)))
    i = jnp.arange(blk, dtype=jnp.float32)
    diff = i[:, None] - i[None, :]
    decay = jnp.where(diff >= 0, jnp.exp(log_g[:, None, None] * jnp.maximum(diff, 0.0)), 0.0)
    inner = jnp.einsum('bnihd,bnjhd->bnhij', q, k) * decay
    y_in = jnp.einsum('bnhij,bnjhe->bnihe', inner, v)
    zeta = jnp.exp(log_g[:, None] * (blk - 1.0 - i)[None, :])
    kv = jnp.einsum('bnjhd,bnjhe,hj->nbhde', k, v, zeta)
    g_blk = jnp.exp(log_g * blk)[None, :, None, None]

    def step(state, kv_n):
        return g_blk * state + kv_n, state

    s_final, s_prev = lax.scan(step, s0.astype(jnp.float32), kv)
    xi = jnp.exp(log_g[:, None] * (i + 1.0)[None, :])
    y_x = jnp.einsum('bnihd,nbhde,hi->bnihe', q, s_prev, xi)
    return (y_in + y_x).reshape(b, t, h, d).astype(dt), s_final.astype(s0.dtype)


def _lin_combine(e1, e2):
    a1, b1 = e1
    a2, b2 = e2
    return a1 * a2, a2 * b1 + b2


def rg_lru(xr, s_conv, s_lru, conv_w, conv_b, w_gate_a, b_gate_a, w_gate_x, b_gate_x, lru_lambda):
    b, t, w = xr.shape
    xpad = jnp.concatenate([s_conv.astype(xr.dtype), xr], axis=1)
    xc = conv_b + sum(xpad[:, j:j + t] * conv_w[j] for j in range(CONV_W))
    new_conv = xpad[:, t:].astype(s_conv.dtype)
    xb = xc.reshape(b, t, N_BLOCKS_C, BLOCK_C)
    r = jax.nn.sigmoid((jnp.einsum('btni,nij->btnj', xb, w_gate_a).reshape(b, t, w) + b_gate_a).astype(jnp.float32))
    ig = jax.nn.sigmoid((jnp.einsum('btni,nij->btnj', xb, w_gate_x).reshape(b, t, w) + b_gate_x).astype(jnp.float32))
    log_a = -LRU_C * r * jax.nn.softplus(-lru_lambda.astype(jnp.float32))
    a = jnp.exp(log_a)
    bx = jnp.sqrt(-jnp.expm1(2.0 * log_a)) * ig * xc.astype(jnp.float32)
    acc_a, acc_b = lax.associative_scan(_lin_combine, (a, bx), axis=1)
    h = acc_a * s_lru.astype(jnp.float32)[:, None, :] + acc_b
    return h.astype(xr.dtype), h[:, -1].astype(s_lru.dtype), new_conv


def layer(x, pe, pos, w_in, rel_table, gn_gain, conv_w, conv_b, w_gate_a, b_gate_a, w_gate_x, b_gate_x,
          lru_lambda, w_branch, w_out, ln_gain, ln_bias, w_ple, w_ple_gate,
          k_cache, v_cache, s_ret, s_conv, s_lru):
    b, t, _ = x.shape
    u = x @ w_in
    qa, ka, va, za, qb, kb, vb, zb, xr, zc, ga, gb, gc = split_cols(u)

    def heads(z):
        return z.reshape(b, t, -1, HEAD_DIM)

    qa, ka, va = heads(qa), heads(ka), heads(va)
    if k_cache is None:
        ya = band_attention_prompt(qa, ka, va, rel_table)
        rows = min(BAND_ROWS, t)
        new_k, new_v = ka[:, t - rows:], va[:, t - rows:]
        ret_blk = CHUNK
    else:
        ya = band_attention_sample(qa, ka, va, k_cache, v_cache, rel_table)
        new_k, new_v = ka, va
        ret_blk = t
    ya = ya.reshape(b, t, W_A) * jax.nn.silu(za)

    qb = rope(heads(qb), pos)
    kb = rope(heads(kb), pos) * (HEAD_DIM ** -0.5)
    yb, new_ret = retention(qb, kb, heads(vb), s_ret, ret_blk)
    yb = group_norm_heads(yb, gn_gain) * jax.nn.silu(zb)

    yc, new_lru, new_conv = rg_lru(xr, s_conv, s_lru, conv_w, conv_b, w_gate_a, b_gate_a,
                                   w_gate_x, b_gate_x, lru_lambda)
    yc = yc * jax.nn.silu(zc)

    merged = (jax.nn.sigmoid(ga) * (ya @ w_branch[0])
              + jax.nn.sigmoid(gb) * (yb @ w_branch[1])
              + jax.nn.sigmoid(gc) * (yc @ w_branch[2]))
    r = ALPHA * x + merged @ w_out
    r = r + jax.nn.sigmoid(r @ w_ple_gate) * (pe @ w_ple)
    return layer_norm(r, ln_gain, ln_bias), (new_k, new_v, new_ret, new_conv, new_lru)


def setup_inputs(seed: int = 0) -> dict:
    key = jax.random.key(seed)
    ks = iter(jax.random.split(key, 32))

    def nrm(shape, scale):
        return jax.random.normal(next(ks), shape, jnp.float32) * scale

    rows = min(BAND_ROWS, PAST_LEN)
    a0 = jax.random.uniform(next(ks), (DEPTH, W_C), jnp.float32, 0.9, 0.999)
    sig = a0 ** (1.0 / LRU_C)
    return {
        'x_prompt': nrm((BATCH, SEQ, D_MODEL), 1.0),
        'x_sample': nrm((DEC_BATCH, DEC_SEQ, D_MODEL), 1.0),
        'p_prompt': nrm((DEPTH, BATCH, SEQ, PLE_DIM), 1.0),
        'p_sample': nrm((DEPTH, DEC_BATCH, DEC_SEQ, PLE_DIM), 1.0),
        'cache_k_a': nrm((DEPTH, DEC_BATCH, rows, N_HEADS_A, HEAD_DIM), 1.0),
        'cache_v_a': nrm((DEPTH, DEC_BATCH, rows, N_HEADS_A, HEAD_DIM), 1.0),
        'state_ret': nrm((DEPTH, DEC_BATCH, N_HEADS_B, HEAD_DIM, HEAD_DIM), 0.5),
        'state_conv': nrm((DEPTH, DEC_BATCH, CONV_W - 1, W_C), 1.0),
        'state_lru': nrm((DEPTH, DEC_BATCH, W_C), 0.5),
        'w_in': nrm((DEPTH, D_MODEL, N_IN), D_MODEL ** -0.5),
        'rel_table': nrm((DEPTH, N_HEADS_A, 2 * REL_CLIP + 1), 0.5),
        'gn_gain': 1.0 + nrm((DEPTH, W_B), 0.1),
        'conv_w': nrm((DEPTH, CONV_W, W_C), CONV_W ** -0.5),
        'conv_b': nrm((DEPTH, W_C), 0.02),
        'w_gate_a': nrm((DEPTH, N_BLOCKS_C, BLOCK_C, BLOCK_C), BLOCK_C ** -0.5),
        'b_gate_a': nrm((DEPTH, W_C), 0.02),
        'w_gate_x': nrm((DEPTH, N_BLOCKS_C, BLOCK_C, BLOCK_C), BLOCK_C ** -0.5),
        'b_gate_x': nrm((DEPTH, W_C), 0.02),
        'lru_lambda': jnp.log(sig) - jnp.log1p(-sig),
        'w_branch': nrm((DEPTH, N_BRANCH, W_A, D_MODEL), (W_A ** -0.5) * BETA),
        'w_out': nrm((DEPTH, D_MODEL, D_MODEL), (D_MODEL ** -0.5) * BETA),
        'ln_gain': 1.0 + nrm((DEPTH, D_MODEL), 0.05),
        'ln_bias': nrm((DEPTH, D_MODEL), 0.02),
        'w_ple': nrm((DEPTH, PLE_DIM, D_MODEL), PLE_DIM ** -0.5),
        'w_ple_gate': nrm((DEPTH, D_MODEL, D_MODEL), D_MODEL ** -0.5),
    }


def reference(x_prompt, x_sample, p_prompt, p_sample, cache_k_a, cache_v_a, state_ret, state_conv,
              state_lru, w_in, rel_table, gn_gain, conv_w, conv_b, w_gate_a, b_gate_a, w_gate_x,
              b_gate_x, lru_lambda, w_branch, w_out, ln_gain, ln_bias, w_ple, w_ple_gate):
    bp, tp = x_prompt.shape[0], x_prompt.shape[1]
    ts = x_sample.shape[1]
    dt = x_prompt.dtype
    pos_p = jnp.arange(tp)
    pos_s = PAST_LEN + jnp.arange(ts)
    hp, hs = x_prompt, x_sample
    kp_l, vp_l, rp_l, cp_l, lp_l = [], [], [], [], []
    ks_l, vs_l, rs_l, cs_l, ls_l = [], [], [], [], []
    for l in range(DEPTH):
        wl = (w_in[l], rel_table[l], gn_gain[l], conv_w[l], conv_b[l], w_gate_a[l], b_gate_a[l],
              w_gate_x[l], b_gate_x[l], lru_lambda[l], w_branch[l], w_out[l], ln_gain[l], ln_bias[l],
              w_ple[l], w_ple_gate[l])
        hp, (k_p, v_p, r_p, c_p, s_p) = layer(
            hp, p_prompt[l], pos_p, *wl, None, None,
            jnp.zeros((bp, N_HEADS_B, HEAD_DIM, HEAD_DIM), dt),
            jnp.zeros((bp, CONV_W - 1, W_C), dt),
            jnp.zeros((bp, W_C), dt))
        hs, (k_s, v_s, r_s, c_s, s_s) = layer(
            hs, p_sample[l], pos_s, *wl, cache_k_a[l], cache_v_a[l],
            state_ret[l], state_conv[l], state_lru[l])
        kp_l.append(k_p); vp_l.append(v_p); rp_l.append(r_p); cp_l.append(c_p); lp_l.append(s_p)
        ks_l.append(k_s); vs_l.append(v_s); rs_l.append(r_s); cs_l.append(c_s); ls_l.append(s_s)
    return (hp, hs,
            jnp.stack(kp_l), jnp.stack(vp_l), jnp.stack(ks_l), jnp.stack(vs_l),
            jnp.stack(rp_l), jnp.stack(rs_l), jnp.stack(cp_l), jnp.stack(cs_l),
            jnp.stack(lp_l), jnp.stack(ls_l))
```

```python
import functools

import jax
import jax.numpy as jnp
from jax import lax
from jax.experimental import pallas as pl
from jax.experimental.pallas import tpu as pltpu

F32 = jnp.float32
BF16 = jnp.bfloat16

CHUNK = 64
N_LEFT_CHUNKS = 8
HIST = N_LEFT_CHUNKS * CHUNK
HEAD_DIM = 64
N_HEADS = 8
W_BRANCH = N_HEADS * HEAD_DIM
N_PAIRS = N_HEADS // 2
LANES = 128
REL_CLIP = 128
CONV_W = 4
LRU_C = 8.0
ROPE_BASE = 10000.0
LN_EPS = 1e-5
NEG_INF = -1e30
PAST_LEN = 1024
N_MIX_COLS = 10 * W_BRANCH
PROMPT_TILE = 256
MERGE_TILE = 256
VMEM_LIMIT_BYTES = 56 * 1024 * 1024

_NT = (((1,), (1,)), ((), ()))
_TN = (((0,), (0,)), ((), ()))


def _const_spec(shape):
    zeros = (0,) * len(shape)
    return pl.BlockSpec(shape, lambda *_: zeros, pipeline_mode=pl.Buffered(1))


def _silu(x):
    return x * jax.nn.sigmoid(x)


def _bias_kernel(tab_ref, bp_ref, bs_ref, *, tq, ts):
    h = pl.program_id(0)

    def toeplitz_block(rows, i0, j0):
        i = lax.broadcasted_iota(jnp.int32, (rows, LANES), 0) + i0
        j = lax.broadcasted_iota(jnp.int32, (rows, LANES), 1) + j0
        idx = jnp.clip(HIST + i - j, -REL_CLIP, REL_CLIP) + REL_CLIP
        d_lo = HIST + i0 - (j0 + LANES - 1)
        d_hi = HIST + i0 + rows - 1 - j0
        k_lo = min(max(d_lo, -REL_CLIP), REL_CLIP) + REL_CLIP
        k_hi = min(max(d_hi, -REL_CLIP), REL_CLIP) + REL_CLIP

        def body(k, acc):
            return jnp.where(idx == k, tab_ref[h, k], acc)

        blk = lax.fori_loop(k_lo, k_hi + 1, body, jnp.zeros((rows, LANES), F32))
        return blk, i, j

    for ib in range(tq // LANES):
        for jb in range((HIST + tq) // LANES):
            blk, i, j = toeplitz_block(LANES, ib * LANES, jb * LANES)
            qc = i // CHUNK
            kc = j // CHUNK
            vis = (kc >= qc) & (kc <= qc + N_LEFT_CHUNKS)
            bp_ref[0, ib * LANES:(ib + 1) * LANES, jb * LANES:(jb + 1) * LANES] = jnp.where(vis, blk, NEG_INF)
    for jb in range(bs_ref.shape[2] // LANES):
        blk, _, _ = toeplitz_block(ts, 0, jb * LANES)
        bs_ref[0, :, jb * LANES:(jb + 1) * LANES] = blk


def _expand_bias(rel_table, tq, ts):
    depth, nh, nrel = rel_table.shape
    ws = -(-(HIST + ts) // LANES) * LANES
    bp, bs = pl.pallas_call(
        functools.partial(_bias_kernel, tq=tq, ts=ts),
        grid=(depth * nh,),
        in_specs=[pl.BlockSpec(memory_space=pltpu.SMEM)],
        out_specs=[pl.BlockSpec((1, tq, HIST + tq), lambda g: (g, 0, 0)),
                   pl.BlockSpec((1, ts, ws), lambda g: (g, 0, 0))],
        out_shape=[jax.ShapeDtypeStruct((depth * nh, tq, HIST + tq), F32),
                   jax.ShapeDtypeStruct((depth * nh, ts, ws), F32)],
        name="rel_bias",
    )(rel_table.reshape(depth * nh, nrel))
    return bp.reshape(depth, nh, tq, HIST + tq), bs.reshape(depth, nh, ts, ws)


def _mixer_kernel(*refs, T, prompt, n_tiles, kv_first):
    if prompt:
        (x_ref, cos_ref, sin_ref, w_ref, bias_ref, dec_ref, xi_ref, zeta_ref, gblk_ref, seg_ref, gn_ref,
         cw_ref, cb_ref, wga_ref, bga_ref, wgx_ref, bgx_ref, lam_ref,
         y_ref, ko_ref, vo_ref, ret_ref, conv_ref, lru_ref,
         xb_s, qa_s, kown_s, vown_s, khist_s, vhist_s, ya_s, yb_s, st_s, xbuf, h_s) = refs
    else:
        (x_ref, cos_ref, sin_ref, w_ref, bias_ref, dec_ref, xi_ref, zeta_ref, gblk_ref, seg_ref, gn_ref,
         cw_ref, cb_ref, wga_ref, bga_ref, wgx_ref, bgx_ref, lam_ref,
         kc_ref, vc_ref, s0_ref, c0_ref, h0_ref,
         y_ref, ko_ref, vo_ref, ret_ref, conv_ref, lru_ref,
         xb_s, qa_s, kown_s, vown_s, khist_s, vhist_s, ya_s, yb_s, st_s, xbuf, h_s) = refs

    t = pl.program_id(1)
    lane = lax.broadcasted_iota(jnp.int32, (T, LANES), 1)
    first_head = lane < HEAD_DIM

    def split_heads(pair):
        zero = jnp.zeros_like(pair)
        m = first_head if pair.shape[0] == T else lax.broadcasted_iota(jnp.int32, pair.shape, 1) < HEAD_DIM
        return jnp.where(m, pair, zero), jnp.where(m, zero, pair)

    if prompt:
        @pl.when(t == 0)
        def _():
            khist_s[...] = jnp.zeros_like(khist_s)
            vhist_s[...] = jnp.zeros_like(vhist_s)
            st_s[...] = jnp.zeros_like(st_s)
            xbuf[0:8, :] = jnp.zeros((8, W_BRANCH), F32)
            h_s[...] = jnp.zeros_like(h_s)
    else:
        kc = kc_ref[0].astype(BF16)
        vc = vc_ref[0].astype(BF16)
        for hp in range(N_PAIRS):
            sl = slice(hp * LANES, (hp + 1) * LANES)
            khist_s[2 * hp], khist_s[2 * hp + 1] = split_heads(kc[:, sl])
            vhist_s[2 * hp], vhist_s[2 * hp + 1] = split_heads(vc[:, sl])
        st_s[...] = s0_ref[0]
        xbuf[0:8, :] = c0_ref[0]
        h_s[...] = h0_ref[0]

    xb_s[...] = x_ref[0].astype(BF16)

    def proj(c):
        return jnp.dot(xb_s[...], w_ref[:, c * W_BRANCH:(c + 1) * W_BRANCH], preferred_element_type=F32)

    qa_s[...] = (proj(0) * (HEAD_DIM ** -0.5)).astype(BF16)
    ka = proj(1)
    va = proj(2)
    if prompt:
        @pl.when(t >= kv_first)
        def _():
            ko_ref[0] = ka
            vo_ref[0] = va
    else:
        ko_ref[0] = ka
        vo_ref[0] = va
    kab = ka.astype(BF16)
    vab = va.astype(BF16)
    for hp in range(N_PAIRS):
        sl = slice(hp * LANES, (hp + 1) * LANES)
        kown_s[2 * hp], kown_s[2 * hp + 1] = split_heads(kab[:, sl])
        vown_s[2 * hp], vown_s[2 * hp + 1] = split_heads(vab[:, sl])

    if prompt:
        col = lax.broadcasted_iota(jnp.int32, (1, HIST), 1)
        start_mask = jnp.where(t * T - HIST + col >= 0, 0.0, NEG_INF).astype(F32)

    for hp in range(N_PAIRS):
        sl = slice(hp * LANES, (hp + 1) * LANES)
        q_p = qa_s[:, sl]
        acc = jnp.zeros((T, LANES), F32)
        for e in range(2):
            h = 2 * hp + e
            s_h = lax.dot_general(q_p, khist_s[h], _NT, preferred_element_type=F32) + bias_ref[h, :, 0:HIST]
            if prompt:
                s_h = s_h + start_mask
            s_o = lax.dot_general(q_p, kown_s[h], _NT, preferred_element_type=F32) + bias_ref[h, :, HIST:HIST + T]
            m = jnp.maximum(jnp.max(s_h, axis=1, keepdims=True), jnp.max(s_o, axis=1, keepdims=True))
            p_h = jnp.exp(s_h - m)
            p_o = jnp.exp(s_o - m)
            l = jnp.sum(p_h, axis=1, keepdims=True) + jnp.sum(p_o, axis=1, keepdims=True)
            o = (jnp.dot(p_h.astype(BF16), vhist_s[h], preferred_element_type=F32)
                 + jnp.dot(p_o.astype(BF16), vown_s[h], preferred_element_type=F32))
            acc = acc + o / l
        ya_s[:, sl] = acc

    if prompt:
        for h in range(N_HEADS):
            if T < HIST:
                khist_s[h, 0:HIST - T] = khist_s[h, T:HIST]
                vhist_s[h, 0:HIST - T] = vhist_s[h, T:HIST]
            khist_s[h, HIST - T:HIST] = kown_s[h]
            vhist_s[h, HIST - T:HIST] = vown_s[h]

    y_ref[0, :, 0:W_BRANCH] = (ya_s[...] * _silu(proj(3))).astype(BF16)

    cos = cos_ref[...]
    sin = sin_ref[...]
    lane_w = lax.broadcasted_iota(jnp.int32, (T, W_BRANCH), 1)
    low_half = (lane_w % HEAD_DIM) < (HEAD_DIM // 2)

    def rope(v):
        partner = jnp.where(low_half, pltpu.roll(v, W_BRANCH - HEAD_DIM // 2, 1), pltpu.roll(v, HEAD_DIM // 2, 1))
        return v * cos + partner * sin

    qr = rope(proj(4))
    kr = rope(proj(5)) * (HEAD_DIM ** -0.5)
    q_in = qr.astype(BF16)
    q_x = (qr * xi_ref[...]).astype(BF16)
    k_in = kr.astype(BF16)
    k_z = (kr * zeta_ref[...]).astype(BF16)
    v_b = proj(6).astype(BF16)
    for hp in range(N_PAIRS):
        sl = slice(hp * LANES, (hp + 1) * LANES)
        k_heads = split_heads(k_in[:, sl])
        kz_heads = split_heads(k_z[:, sl])
        v_heads = split_heads(v_b[:, sl])
        acc = jnp.zeros((T, LANES), F32)
        for e in range(2):
            h = 2 * hp + e
            qk = lax.dot_general(q_in[:, sl], k_heads[e], _NT, preferred_element_type=F32)
            inner = (qk * dec_ref[h]).astype(BF16)
            state = st_s[h]
            acc = acc + jnp.dot(inner, v_heads[e], preferred_element_type=F32)
            acc = acc + jnp.dot(q_x[:, sl], state.astype(BF16), preferred_element_type=F32)
            st_s[h] = gblk_ref[h] * state + lax.dot_general(kz_heads[e], v_heads[e], _TN, preferred_element_type=F32)
        yb_s[:, sl] = acc

    def seg_mean(a):
        hi = a.astype(BF16)
        lo = (a - hi.astype(F32)).astype(BF16)
        return (jnp.dot(hi, seg_ref[...], preferred_element_type=F32)
                + jnp.dot(lo, seg_ref[...], preferred_element_type=F32))

    yb = yb_s[...]
    dev = yb - seg_mean(yb)
    var = seg_mean(dev * dev)
    ybn = dev * lax.rsqrt(var + LN_EPS) * gn_ref[...]
    y_ref[0, :, W_BRANCH:2 * W_BRANCH] = (ybn * _silu(proj(7))).astype(BF16)

    xr = proj(8)
    xbuf[8:8 + T, :] = xr
    xc = cb_ref[...] + xr * cw_ref[CONV_W - 1:CONV_W, :]
    for j in range(CONV_W - 1):
        xc = xc + xbuf[8 - (CONV_W - 1) + j:8 - (CONV_W - 1) + j + T, :] * cw_ref[j:j + 1, :]
    xcb = xc.astype(BF16)
    r = jax.nn.sigmoid(jnp.dot(xcb, wga_ref[...], preferred_element_type=F32) + bga_ref[...])
    ig = jax.nn.sigmoid(jnp.dot(xcb, wgx_ref[...], preferred_element_type=F32) + bgx_ref[...])
    log_a = (-LRU_C) * r * jax.nn.softplus(-lam_ref[...])
    a_cum = jnp.exp(log_a)
    b_cum = jnp.sqrt(1.0 - a_cum * a_cum) * ig * xc
    row = lax.broadcasted_iota(jnp.int32, (T, W_BRANCH), 0)
    step = 1
    while step < T:
        keep = row >= step
        a_prev = jnp.where(keep, pltpu.roll(a_cum, step, 0), 1.0)
        b_prev = jnp.where(keep, pltpu.roll(b_cum, step, 0), 0.0)
        b_cum = a_cum * b_prev + b_cum
        a_cum = a_cum * a_prev
        step *= 2
    hseq = a_cum * h_s[...] + b_cum
    h_s[...] = hseq[T - 1:T, :]
    y_ref[0, :, 2 * W_BRANCH:3 * W_BRANCH] = (hseq * _silu(proj(9))).astype(BF16)
    xbuf[0:8, :] = xbuf[T:T + 8, :]

    def write_state():
        ret_ref[0] = st_s[...]
        conv_ref[0] = xbuf[0:8, :]
        lru_ref[0] = h_s[...]

    if prompt:
        pl.when(t == n_tiles - 1)(write_state)
    else:
        write_state()


def _mixer_call(x, cos, sin, w_mix, bias, consts, layer_w, cache, *, T, prompt):
    B, S, D = x.shape
    n_tiles = S // T
    kv_rows = min(HIST, S)
    kv_first = n_tiles - kv_rows // T
    dec, xi, zeta, gblk, seg = consts
    gn, cw, cb, wga, bga, wgx, bgx, lam = layer_w

    in_specs = [
        pl.BlockSpec((1, T, D), lambda b, t: (b, t, 0)),
        pl.BlockSpec((T, W_BRANCH), lambda b, t: (t, 0)),
        pl.BlockSpec((T, W_BRANCH), lambda b, t: (t, 0)),
        _const_spec(w_mix.shape), _const_spec(bias.shape), _const_spec(dec.shape),
        _const_spec(xi.shape), _const_spec(zeta.shape),
        pl.BlockSpec(memory_space=pltpu.SMEM),
        _const_spec(seg.shape), _const_spec(gn.shape), _const_spec(cw.shape), _const_spec(cb.shape),
        _const_spec(wga.shape), _const_spec(bga.shape), _const_spec(wgx.shape), _const_spec(bgx.shape),
        _const_spec(lam.shape),
    ]
    args = [x, cos, sin, w_mix, bias, dec, xi, zeta, gblk, seg, gn, cw, cb, wga, bga, wgx, bgx, lam]
    if not prompt:
        kc, vc, s0, c0, h0 = cache
        in_specs += [
            pl.BlockSpec((1, HIST, W_BRANCH), lambda b, t: (b, 0, 0)),
            pl.BlockSpec((1, HIST, W_BRANCH), lambda b, t: (b, 0, 0)),
            pl.BlockSpec((1, N_HEADS, LANES, LANES), lambda b, t: (b, 0, 0, 0)),
            pl.BlockSpec((1, 8, W_BRANCH), lambda b, t: (b, 0, 0)),
            pl.BlockSpec((1, 1, W_BRANCH), lambda b, t: (b, 0, 0)),
        ]
        args += [kc, vc, s0, c0, h0]

    def kv_map(b, t):
        return (b, jnp.maximum(t - kv_first, 0), 0)

    out_specs = [
        pl.BlockSpec((1, T, 3 * W_BRANCH), lambda b, t: (b, t, 0)),
        pl.BlockSpec((1, T, W_BRANCH), kv_map),
        pl.BlockSpec((1, T, W_BRANCH), kv_map),
        pl.BlockSpec((1, N_HEADS, LANES, LANES), lambda b, t: (b, 0, 0, 0)),
        pl.BlockSpec((1, 8, W_BRANCH), lambda b, t: (b, 0, 0)),
        pl.BlockSpec((1, 1, W_BRANCH), lambda b, t: (b, 0, 0)),
    ]
    out_shape = [
        jax.ShapeDtypeStruct((B, S, 3 * W_BRANCH), BF16),
        jax.ShapeDtypeStruct((B, kv_rows, W_BRANCH), F32),
        jax.ShapeDtypeStruct((B, kv_rows, W_BRANCH), F32),
        jax.ShapeDtypeStruct((B, N_HEADS, LANES, LANES), F32),
        jax.ShapeDtypeStruct((B, 8, W_BRANCH), F32),
        jax.ShapeDtypeStruct((B, 1, W_BRANCH), F32),
    ]
    scratch = [
        pltpu.VMEM((T, D), BF16),
        pltpu.VMEM((T, W_BRANCH), BF16),
        pltpu.VMEM((N_HEADS, T, LANES), BF16),
        pltpu.VMEM((N_HEADS, T, LANES), BF16),
        pltpu.VMEM((N_HEADS, HIST, LANES), BF16),
        pltpu.VMEM((N_HEADS, HIST, LANES), BF16),
        pltpu.VMEM((T, W_BRANCH), F32),
        pltpu.VMEM((T, W_BRANCH), F32),
        pltpu.VMEM((N_HEADS, LANES, LANES), F32),
        pltpu.VMEM((T + 8, W_BRANCH), F32),
        pltpu.VMEM((1, W_BRANCH), F32),
    ]
    return pl.pallas_call(
        functools.partial(_mixer_kernel, T=T, prompt=prompt, n_tiles=n_tiles, kv_first=kv_first),
        grid=(B, n_tiles),
        in_specs=in_specs,
        out_specs=out_specs,
        out_shape=out_shape,
        scratch_shapes=scratch,
        compiler_params=pltpu.CompilerParams(
            dimension_semantics=("arbitrary", "arbitrary"), vmem_limit_bytes=VMEM_LIMIT_BYTES),
        name="mixer_prompt" if prompt else "mixer_sample",
    )(*args)


def _merge_kernel(x_ref, y_ref, p_ref, wg_ref, wb_ref, wo_ref, wpg_ref, wpe_ref, lng_ref, lnb_ref, o_ref, *, alpha):
    x = x_ref[...]
    xb = x.astype(BF16)
    d = x.shape[1]
    merged = None
    for b in range(3):
        gate = jax.nn.sigmoid(jnp.dot(xb, wg_ref[:, b * d:(b + 1) * d], preferred_element_type=F32))
        term = gate * jnp.dot(y_ref[:, b * W_BRANCH:(b + 1) * W_BRANCH], wb_ref[b], preferred_element_type=F32)
        merged = term if merged is None else merged + term
    r = alpha * x + jnp.dot(merged.astype(BF16), wo_ref[...], preferred_element_type=F32)
    pgate = jax.nn.sigmoid(jnp.dot(r.astype(BF16), wpg_ref[...], preferred_element_type=F32))
    r = r + pgate * jnp.dot(p_ref[...].astype(BF16), wpe_ref[...], preferred_element_type=F32)
    mu = jnp.mean(r, axis=-1, keepdims=True)
    dev = r - mu
    var = jnp.mean(dev * dev, axis=-1, keepdims=True)
    o_ref[...] = dev * lax.rsqrt(var + LN_EPS) * lng_ref[...] + lnb_ref[...]


def _merge_call(x2, y2, p2, wg, wb, wo, wpg, wpe, lng, lnb, *, alpha):
    n, d = x2.shape
    tm = min(MERGE_TILE, n)
    return pl.pallas_call(
        functools.partial(_merge_kernel, alpha=alpha),
        grid=(n // tm,),
        in_specs=[
            pl.BlockSpec((tm, d), lambda i: (i, 0)),
            pl.BlockSpec((tm, y2.shape[1]), lambda i: (i, 0)),
            pl.BlockSpec((tm, p2.shape[1]), lambda i: (i, 0)),
            _const_spec(wg.shape), _const_spec(wb.shape), _const_spec(wo.shape), _const_spec(wpg.shape),
            _const_spec(wpe.shape), _const_spec(lng.shape), _const_spec(lnb.shape),
        ],
        out_specs=pl.BlockSpec((tm, d), lambda i: (i, 0)),
        out_shape=jax.ShapeDtypeStruct((n, d), F32),
        compiler_params=pltpu.CompilerParams(
            dimension_semantics=("arbitrary",), vmem_limit_bytes=VMEM_LIMIT_BYTES),
        name="merge",
    )(x2, y2, p2, wg, wb, wo, wpg, wpe, lng, lnb)


def _rope_tables(pos):
    half = HEAD_DIM // 2
    inv = ROPE_BASE ** (-jnp.arange(half, dtype=F32) / half)
    ang = pos.astype(F32)[:, None] * inv[None, :]
    c = jnp.cos(ang)
    s = jnp.sin(ang)
    cos = jnp.tile(jnp.concatenate([c, c], axis=-1), (1, N_HEADS))
    sin = jnp.tile(jnp.concatenate([-s, s], axis=-1), (1, N_HEADS))
    return cos, sin


def _retention_tables(blk):
    log_g = jnp.log1p(-jnp.exp2(-5.0 - jnp.arange(N_HEADS, dtype=F32)))
    i = jnp.arange(blk, dtype=F32)
    diff = i[:, None] - i[None, :]
    decay = jnp.where(diff >= 0, jnp.exp(log_g[:, None, None] * jnp.maximum(diff, 0.0)), 0.0)
    zeta = jnp.exp(log_g[:, None] * (blk - 1.0 - i)[None, :])
    xi = jnp.exp(log_g[:, None] * (i + 1.0)[None, :])
    g_blk = jnp.exp(log_g * blk)
    xi_full = jnp.repeat(xi.T, HEAD_DIM, axis=1)
    zeta_full = jnp.repeat(zeta.T, HEAD_DIM, axis=1)
    lane = jnp.arange(W_BRANCH)
    seg = jnp.where(lane[:, None] // HEAD_DIM == lane[None, :] // HEAD_DIM, 1.0 / HEAD_DIM, 0.0).astype(BF16)
    return decay, xi_full, zeta_full, g_blk, seg


def _block_diag(w):
    n, bi, bj = w.shape
    out = jnp.zeros((n * bi, n * bj), w.dtype)
    for k in range(n):
        out = lax.dynamic_update_slice(out, w[k], (k * bi, k * bj))
    return out


def _pad_states(s_ret, s_conv, s_lru):
    b = s_ret.shape[0]
    pad = jnp.zeros((b, N_HEADS, LANES, LANES), F32)
    pad = pad.at[:, 0::2, :HEAD_DIM, :HEAD_DIM].set(s_ret[:, 0::2])
    pad = pad.at[:, 1::2, HEAD_DIM:, HEAD_DIM:].set(s_ret[:, 1::2])
    conv = jnp.pad(s_conv, ((0, 0), (8 - (CONV_W - 1), 0), (0, 0)))
    return pad, conv, s_lru[:, None, :]


def _unpad_states(ret, conv, lru):
    even = ret[:, 0::2, :HEAD_DIM, :HEAD_DIM]
    odd = ret[:, 1::2, HEAD_DIM:, HEAD_DIM:]
    full = jnp.stack([even, odd], axis=2).reshape(ret.shape[0], N_HEADS, HEAD_DIM, HEAD_DIM)
    return full, conv[:, 8 - (CONV_W - 1):, :], lru[:, 0, :]


def kernel(x_prompt, x_sample, p_prompt, p_sample, cache_k_a, cache_v_a, state_ret, state_conv, state_lru, w_in, rel_table, gn_gain, conv_w, conv_b, w_gate_a, b_gate_a, w_gate_x, b_gate_x, lru_lambda, w_branch, w_out, ln_gain, ln_bias, w_ple, w_ple_gate):
    depth = w_in.shape[0]
    bp, sp, d_model = x_prompt.shape
    bs, ss, _ = x_sample.shape
    assert cache_k_a.shape[2] == HIST and w_in.shape[2] == N_MIX_COLS + 3 * d_model
    tp = min(PROMPT_TILE, sp)
    assert sp % tp == 0 and tp % LANES == 0 and HIST % tp == 0 and ss % 8 == 0
    alpha = float((2 * depth) ** 0.25)

    w_in_b = w_in.astype(BF16)
    w_branch_b = w_branch.astype(BF16)
    w_out_b = w_out.astype(BF16)
    w_pg_b = w_ple_gate.astype(BF16)
    w_pe_b = w_ple.astype(BF16)

    bias_p, bias_s = _expand_bias(rel_table, tp, ss)
    cos_p, sin_p = _rope_tables(jnp.arange(sp))
    cos_s, sin_s = _rope_tables(PAST_LEN + jnp.arange(ss))
    consts_p = _retention_tables(tp)
    consts_s = _retention_tables(ss)

    hp, hs = x_prompt, x_sample
    outs = [[] for _ in range(10)]
    for l in range(depth):
        w_mix = w_in_b[l, :, :N_MIX_COLS]
        w_gates = w_in_b[l, :, N_MIX_COLS:]
        layer_w = (gn_gain[l][None], conv_w[l], conv_b[l][None],
                   _block_diag(w_gate_a[l]).astype(BF16), b_gate_a[l][None],
                   _block_diag(w_gate_x[l]).astype(BF16), b_gate_x[l][None], lru_lambda[l][None])
        merge_w = (w_gates, w_branch_b[l], w_out_b[l], w_pg_b[l], w_pe_b[l], ln_gain[l][None], ln_bias[l][None])

        y, k_p, v_p, ret, conv, lru = _mixer_call(hp, cos_p, sin_p, w_mix, bias_p[l], consts_p, layer_w, None,
                                                  T=tp, prompt=True)
        r_p, c_p, s_p = _unpad_states(ret, conv, lru)
        hp = _merge_call(hp.reshape(bp * sp, d_model), y.reshape(bp * sp, -1),
                         p_prompt[l].reshape(bp * sp, -1), *merge_w, alpha=alpha).reshape(bp, sp, d_model)

        cache = (cache_k_a[l].reshape(bs, HIST, W_BRANCH), cache_v_a[l].reshape(bs, HIST, W_BRANCH),
                 *_pad_states(state_ret[l], state_conv[l], state_lru[l]))
        y, k_s, v_s, ret, conv, lru = _mixer_call(hs, cos_s, sin_s, w_mix, bias_s[l], consts_s, layer_w, cache,
                                                  T=ss, prompt=False)
        r_s, c_s, s_s = _unpad_states(ret, conv, lru)
        hs = _merge_call(hs.reshape(bs * ss, d_model), y.reshape(bs * ss, -1),
                         p_sample[l].reshape(bs * ss, -1), *merge_w, alpha=alpha).reshape(bs, ss, d_model)

        kv_rows = k_p.shape[1]
        vals = (k_p.reshape(bp, kv_rows, N_HEADS, HEAD_DIM), v_p.reshape(bp, kv_rows, N_HEADS, HEAD_DIM),
                k_s.reshape(bs, ss, N_HEADS, HEAD_DIM), v_s.reshape(bs, ss, N_HEADS, HEAD_DIM),
                r_p, r_s, c_p, c_s, s_p, s_s)
        for acc, v in zip(outs, vals):
            acc.append(v)
    return (hp, hs) + tuple(jnp.stack(o) for o in outs)
```

```python
import functools

import numpy as np

import jax
import jax.numpy as jnp
from jax import lax
from jax.experimental import pallas as pl
from jax.experimental.pallas import tpu as pltpu

F32 = jnp.float32
BF16 = jnp.bfloat16

CHUNK = 64
N_LEFT_CHUNKS = 8
HIST = N_LEFT_CHUNKS * CHUNK
HEAD_DIM = 64
N_HEADS = 8
W_BRANCH = N_HEADS * HEAD_DIM
N_PAIRS = N_HEADS // 2
LANES = 128
SUBLANES = 8
QUAD = 4 * HEAD_DIM
N_QUADS = N_HEADS // 4
REL_CLIP = 128
CONV_W = 4
LRU_C = 8.0
ROPE_BASE = 10000.0
LN_EPS = 1e-5
NEG_INF = -1e30
LOG2E = 1.4426950408889634
PAST_LEN = 1024
N_MIX_COLS = 10 * W_BRANCH
PROMPT_TILE = 256
MERGE_TILE = 256
SOFTMAX_ROWS = 16
VMEM_LIMIT_BYTES = 56 * 1024 * 1024

_NT = (((1,), (1,)), ((), ()))
_TN = (((0,), (0,)), ((), ()))


def _const_spec(shape):
    zeros = (0,) * len(shape)
    return pl.BlockSpec(shape, lambda *_: zeros, pipeline_mode=pl.Buffered(1))


def _layer_spec(arr, layer, block=None, index=None):
    block = tuple(arr.shape[1:]) if block is None else tuple(block)
    index = (0,) * len(block) if index is None else tuple(index)
    return pl.BlockSpec((None,) + block, lambda *_: (layer,) + index, pipeline_mode=pl.Buffered(1))


def _sigmoid(x):
    return 0.5 * jnp.tanh(0.5 * x) + 0.5


def _silu(x):
    half = 0.5 * x
    return half * jnp.tanh(half) + half


def _bias_kernel(tab_ref, bp_ref, bs_ref, *, tq, ts):
    h = pl.program_id(0)

    def toeplitz_block(rows, i0, j0):
        i = lax.broadcasted_iota(jnp.int32, (rows, LANES), 0) + i0
        j = lax.broadcasted_iota(jnp.int32, (rows, LANES), 1) + j0
        idx = jnp.clip(HIST + i - j, -REL_CLIP, REL_CLIP) + REL_CLIP
        d_lo = HIST + i0 - (j0 + LANES - 1)
        d_hi = HIST + i0 + rows - 1 - j0
        k_lo = min(max(d_lo, -REL_CLIP), REL_CLIP) + REL_CLIP
        k_hi = min(max(d_hi, -REL_CLIP), REL_CLIP) + REL_CLIP

        def body(k, acc):
            return jnp.where(idx == k, tab_ref[h, k], acc)

        blk = lax.fori_loop(k_lo, k_hi + 1, body, jnp.zeros((rows, LANES), F32))
        return blk * LOG2E, i, j

    for ib in range(tq // LANES):
        for jb in range((HIST + tq) // LANES):
            blk, i, j = toeplitz_block(LANES, ib * LANES, jb * LANES)
            qc = i // CHUNK
            kc = j // CHUNK
            vis = (kc >= qc) & (kc <= qc + N_LEFT_CHUNKS)
            bp_ref[0, ib * LANES:(ib + 1) * LANES, jb * LANES:(jb + 1) * LANES] = jnp.where(vis, blk, NEG_INF)
    for jb in range(bs_ref.shape[2] // LANES):
        blk, _, _ = toeplitz_block(ts, 0, jb * LANES)
        bs_ref[0, :, jb * LANES:(jb + 1) * LANES] = blk


def _expand_bias(rel_table, tq, ts):
    depth, nh, nrel = rel_table.shape
    ws = -(-(HIST + ts) // LANES) * LANES
    bp, bs = pl.pallas_call(
        functools.partial(_bias_kernel, tq=tq, ts=ts),
        grid=(depth * nh,),
        in_specs=[pl.BlockSpec(memory_space=pltpu.SMEM)],
        out_specs=[pl.BlockSpec((1, tq, HIST + tq), lambda g: (g, 0, 0)),
                   pl.BlockSpec((1, ts, ws), lambda g: (g, 0, 0))],
        out_shape=[jax.ShapeDtypeStruct((depth * nh, tq, HIST + tq), F32),
                   jax.ShapeDtypeStruct((depth * nh, ts, ws), F32)],
        name="rel_bias",
    )(rel_table.reshape(depth * nh, nrel))
    return bp.reshape(depth, nh, tq, HIST + tq), bs.reshape(depth, nh, ts, ws)


def _mixer_kernel(*refs, T, prompt, n_tiles, kv_first, n_alias):
    n_in = 19 if prompt else 24
    (x_ref, cos_ref, sin_ref, w_ref, bias_ref, dec_ref, xi_ref, zeta_ref, gmat_ref, bmask_ref, seg_ref, gn_ref,
     cw_ref, cb_ref, wga_ref, bga_ref, wgx_ref, bgx_ref, lam_ref) = refs[:19]
    if not prompt:
        kc_ref, vc_ref, s0_ref, c0_ref, h0_ref = refs[19:24]
    refs = refs[n_in + n_alias:]
    (y_ref, ko_ref, vo_ref, ret_ref, conv_ref, lru_ref,
     xb_s, qa_s, kown_s, vown_s, khist_s, vhist_s, ya_s, yb_s, st_s, xbuf, hseq_s, h_s,
     sc_s, pr_s, rl_s) = refs

    t = pl.program_id(1)
    tail = CONV_W - 1
    lane = lax.broadcasted_iota(jnp.int32, (T, LANES), 1)
    first_head = lane < HEAD_DIM

    def split_heads(pair):
        zero = jnp.zeros_like(pair)
        m = first_head if pair.shape[0] == T else lax.broadcasted_iota(jnp.int32, pair.shape, 1) < HEAD_DIM
        return jnp.where(m, pair, zero), jnp.where(m, zero, pair)

    if prompt:
        @pl.when(t == 0)
        def _():
            khist_s[...] = jnp.zeros_like(khist_s)
            vhist_s[...] = jnp.zeros_like(vhist_s)
            st_s[...] = jnp.zeros_like(st_s)
            xbuf[0:SUBLANES, :] = jnp.zeros((SUBLANES, W_BRANCH), F32)
            h_s[...] = jnp.zeros_like(h_s)
    else:
        kc = kc_ref[0].astype(BF16)
        vc = vc_ref[0].astype(BF16)
        for hp in range(N_PAIRS):
            sl = slice(hp * LANES, (hp + 1) * LANES)
            khist_s[2 * hp], khist_s[2 * hp + 1] = split_heads(kc[:, sl])
            vhist_s[2 * hp], vhist_s[2 * hp + 1] = split_heads(vc[:, sl])
        st_s[...] = jnp.zeros_like(st_s)
        for h in range(N_HEADS):
            d0 = (h % 4) * HEAD_DIM
            st_s[h // 4, d0:d0 + HEAD_DIM, d0:d0 + HEAD_DIM] = s0_ref[0, h]
        xbuf[0:SUBLANES, :] = jnp.zeros((SUBLANES, W_BRANCH), F32)
        xbuf[SUBLANES - tail:SUBLANES, :] = c0_ref[0]
        h_s[...] = h0_ref[0]

    xb_s[...] = x_ref[0].astype(BF16)

    def proj(c):
        return jnp.dot(xb_s[...], w_ref[:, c * W_BRANCH:(c + 1) * W_BRANCH], preferred_element_type=F32)

    xr = proj(8)
    xbuf[SUBLANES:SUBLANES + T, :] = xr
    xc = cb_ref[...] + xr * cw_ref[tail:CONV_W, :]
    for j in range(tail):
        xc = xc + xbuf[SUBLANES - tail + j:SUBLANES - tail + j + T, :] * cw_ref[j:j + 1, :]
    xcb = xc.astype(BF16)
    r = _sigmoid(jnp.dot(xcb, wga_ref[...], preferred_element_type=F32) + bga_ref[...])
    ig = _sigmoid(jnp.dot(xcb, wgx_ref[...], preferred_element_type=F32) + bgx_ref[...])
    log_a = (-LRU_C) * r * jax.nn.softplus(-lam_ref[...])
    a_t = jnp.exp(log_a)
    b_t = jnp.sqrt(1.0 - a_t * a_t) * ig * xc
    n_groups = T // SUBLANES
    a3 = a_t.reshape(n_groups, SUBLANES, W_BRANCH)
    b3 = b_t.reshape(n_groups, SUBLANES, W_BRANCH)
    sub = lax.broadcasted_iota(jnp.int32, (n_groups, SUBLANES, W_BRANCH), 1)
    step = 1
    while step < SUBLANES:
        keep = sub >= step
        a_prev = jnp.where(keep, pltpu.roll(a3, step, 1), 1.0)
        b_prev = jnp.where(keep, pltpu.roll(b3, step, 1), 0.0)
        b3 = a3 * b_prev + b3
        a3 = a3 * a_prev
        step *= 2
    carry = h_s[...]
    for g in range(n_groups):
        hg = a3[g] * carry + b3[g]
        hseq_s[g * SUBLANES:(g + 1) * SUBLANES, :] = hg
        carry = hg[SUBLANES - 1:SUBLANES, :]
    h_s[...] = carry
    y_ref[0, :, 2 * W_BRANCH:3 * W_BRANCH] = (hseq_s[...] * _silu(proj(9))).astype(BF16)
    xbuf[0:SUBLANES, :] = xbuf[T:T + SUBLANES, :]

    qa_s[...] = (proj(0) * (HEAD_DIM ** -0.5 * LOG2E)).astype(BF16)
    ka = proj(1)
    va = proj(2)
    if prompt:
        @pl.when(t >= kv_first)
        def _():
            ko_ref[0] = ka
            vo_ref[0] = va
    else:
        ko_ref[0] = ka
        vo_ref[0] = va
    kab = ka.astype(BF16)
    vab = va.astype(BF16)
    for hp in range(N_PAIRS):
        sl = slice(hp * LANES, (hp + 1) * LANES)
        kown_s[2 * hp], kown_s[2 * hp + 1] = split_heads(kab[:, sl])
        vown_s[2 * hp], vown_s[2 * hp + 1] = split_heads(vab[:, sl])

    def attend(mask_start):
        if mask_start:
            col = lax.broadcasted_iota(jnp.int32, (1, HIST), 1)
            start_mask = jnp.where(t * T - HIST + col >= 0, 0.0, NEG_INF).astype(F32)

        def scores(h):
            sl = slice((h // 2) * LANES, (h // 2 + 1) * LANES)
            q_p = qa_s[:, sl]
            s_h = lax.dot_general(q_p, khist_s[h], _NT, preferred_element_type=F32) + bias_ref[h, :, 0:HIST]
            if mask_start:
                s_h = s_h + start_mask
            sc_s[h % 2, :, 0:HIST] = s_h
            sc_s[h % 2, :, HIST:HIST + T] = (lax.dot_general(q_p, kown_s[h], _NT, preferred_element_type=F32)
                                             + bias_ref[h, :, HIST:HIST + T])

        def softmax(h):
            for r0 in range(0, T, SOFTMAX_ROWS):
                rows = slice(r0, r0 + SOFTMAX_ROWS)
                s = sc_s[h % 2, rows, :]
                p = jnp.exp2(s - jnp.max(s, axis=1, keepdims=True))
                pr_s[h % 2, rows, :] = p.astype(BF16)
                rl_s[h % 2, rows, :] = jnp.broadcast_to(1.0 / jnp.sum(p, axis=1, keepdims=True),
                                                        (SOFTMAX_ROWS, LANES))

        def weighted_values(h):
            sl = slice((h // 2) * LANES, (h // 2 + 1) * LANES)
            o = (jnp.dot(pr_s[h % 2, :, 0:HIST], vhist_s[h], preferred_element_type=F32)
                 + jnp.dot(pr_s[h % 2, :, HIST:HIST + T], vown_s[h], preferred_element_type=F32))
            o = o * rl_s[h % 2]
            ya_s[:, sl] = o if h % 2 == 0 else ya_s[:, sl] + o

        scores(0)
        for h in range(N_HEADS):
            if h + 1 < N_HEADS:
                scores(h + 1)
            softmax(h)
            weighted_values(h)

    if prompt:
        pl.when(t < HIST // T)(functools.partial(attend, True))
        pl.when(t >= HIST // T)(functools.partial(attend, False))
        for h in range(N_HEADS):
            if T < HIST:
                khist_s[h, 0:HIST - T] = khist_s[h, T:HIST]
                vhist_s[h, 0:HIST - T] = vhist_s[h, T:HIST]
            khist_s[h, HIST - T:HIST] = kown_s[h]
            vhist_s[h, HIST - T:HIST] = vown_s[h]
    else:
        attend(False)

    y_ref[0, :, 0:W_BRANCH] = (ya_s[...] * _silu(proj(3))).astype(BF16)

    cos = cos_ref[...]
    sin = sin_ref[...]
    lane_w = lax.broadcasted_iota(jnp.int32, (T, W_BRANCH), 1)
    low_half = (lane_w % HEAD_DIM) < (HEAD_DIM // 2)

    def rope(v):
        partner = jnp.where(low_half, pltpu.roll(v, W_BRANCH - HEAD_DIM // 2, 1), pltpu.roll(v, HEAD_DIM // 2, 1))
        return v * cos + partner * sin

    qr = rope(proj(4))
    kr = rope(proj(5)) * (HEAD_DIM ** -0.5)
    q_in = qr.astype(BF16)
    q_x = (qr * xi_ref[...]).astype(BF16)
    k_in = kr.astype(BF16)
    k_z = (kr * zeta_ref[...]).astype(BF16)
    v_b = proj(6).astype(BF16)
    for g in range(N_QUADS):
        gs = slice(g * QUAD, (g + 1) * QUAD)
        state = st_s[g]
        yb_s[:, gs] = jnp.dot(q_x[:, gs], state.astype(BF16), preferred_element_type=F32)
        outer = lax.dot_general(k_z[:, gs], v_b[:, gs], _TN, preferred_element_type=F32)
        st_s[g] = gmat_ref[g] * state + bmask_ref[...] * outer
    for hp in range(N_PAIRS):
        sl = slice(hp * LANES, (hp + 1) * LANES)
        k_heads = split_heads(k_in[:, sl])
        v_heads = split_heads(v_b[:, sl])
        acc = yb_s[:, sl]
        for e in range(2):
            h = 2 * hp + e
            qk = lax.dot_general(q_in[:, sl], k_heads[e], _NT, preferred_element_type=F32)
            inner = (qk * dec_ref[h]).astype(BF16)
            acc = acc + jnp.dot(inner, v_heads[e], preferred_element_type=F32)
        yb_s[:, sl] = acc

    def seg_mean(a):
        return jnp.dot(a.astype(BF16), seg_ref[...], preferred_element_type=F32)

    yb = yb_s[...]
    dev = yb - seg_mean(yb)
    var = seg_mean(dev * dev)
    ybn = dev * lax.rsqrt(var + LN_EPS) * gn_ref[...]
    y_ref[0, :, W_BRANCH:2 * W_BRANCH] = (ybn * _silu(proj(7))).astype(BF16)

    def write_state():
        for h in range(N_HEADS):
            d0 = (h % 4) * HEAD_DIM
            ret_ref[0, h] = st_s[h // 4, d0:d0 + HEAD_DIM, d0:d0 + HEAD_DIM]
        conv_ref[0] = xbuf[SUBLANES - tail:SUBLANES, :]
        lru_ref[0] = h_s[...]

    if prompt:
        pl.when(t == n_tiles - 1)(write_state)
    else:
        write_state()


def _mixer_call(layer, x, tables, w_in_b, bias, layer_w, cache, prev, *, T, prompt):
    B, S, D = x.shape
    depth = w_in_b.shape[0]
    n_tiles = S // T
    kv_rows = min(HIST, S)
    kv_first = n_tiles - kv_rows // T
    cos, sin, dec, xi, zeta, gmat, bmask, seg = tables
    gn, cw, cb, wga, bga, wgx, bgx, lam = layer_w

    in_specs = [
        pl.BlockSpec((1, T, D), lambda b, t: (b, t, 0)),
        pl.BlockSpec((T, W_BRANCH), lambda b, t: (t, 0)),
        pl.BlockSpec((T, W_BRANCH), lambda b, t: (t, 0)),
        _layer_spec(w_in_b, layer, block=(D, N_MIX_COLS)),
        _layer_spec(bias, layer),
        _const_spec(dec.shape), _const_spec(xi.shape), _const_spec(zeta.shape),
        _const_spec(gmat.shape), _const_spec(bmask.shape), _const_spec(seg.shape),
    ] + [_layer_spec(a, layer) for a in layer_w]
    args = [x, cos, sin, w_in_b, bias, dec, xi, zeta, gmat, bmask, seg, gn, cw, cb, wga, bga, wgx, bgx, lam]
    if not prompt:
        for a in cache:
            blk = (1,) + tuple(a.shape[2:])
            zeros = (0,) * (len(blk) - 1)
            in_specs.append(pl.BlockSpec((None,) + blk, lambda b, t, zeros=zeros: (layer, b) + zeros))
        args += list(cache)
    aliases = {}
    if prev is not None:
        for k, a in enumerate(prev):
            in_specs.append(pl.BlockSpec(memory_space=pl.ANY))
            aliases[len(args)] = k + 1
            args.append(a)

    def kv_map(b, t):
        return (layer, b, jnp.maximum(t - kv_first, 0), 0)

    out_specs = [
        pl.BlockSpec((1, T, 3 * W_BRANCH), lambda b, t: (b, t, 0)),
        pl.BlockSpec((None, 1, T, W_BRANCH), kv_map),
        pl.BlockSpec((None, 1, T, W_BRANCH), kv_map),
        pl.BlockSpec((None, 1, N_HEADS, HEAD_DIM, HEAD_DIM), lambda b, t: (layer, b, 0, 0, 0)),
        pl.BlockSpec((None, 1, CONV_W - 1, W_BRANCH), lambda b, t: (layer, b, 0, 0)),
        pl.BlockSpec((None, 1, 1, W_BRANCH), lambda b, t: (layer, b, 0, 0)),
    ]
    out_shape = [
        jax.ShapeDtypeStruct((B, S, 3 * W_BRANCH), BF16),
        jax.ShapeDtypeStruct((depth, B, kv_rows, W_BRANCH), F32),
        jax.ShapeDtypeStruct((depth, B, kv_rows, W_BRANCH), F32),
        jax.ShapeDtypeStruct((depth, B, N_HEADS, HEAD_DIM, HEAD_DIM), F32),
        jax.ShapeDtypeStruct((depth, B, CONV_W - 1, W_BRANCH), F32),
        jax.ShapeDtypeStruct((depth, B, 1, W_BRANCH), F32),
    ]
    scratch = [
        pltpu.VMEM((T, D), BF16),
        pltpu.VMEM((T, W_BRANCH), BF16),
        pltpu.VMEM((N_HEADS, T, LANES), BF16),
        pltpu.VMEM((N_HEADS, T, LANES), BF16),
        pltpu.VMEM((N_HEADS, HIST, LANES), BF16),
        pltpu.VMEM((N_HEADS, HIST, LANES), BF16),
        pltpu.VMEM((T, W_BRANCH), F32),
        pltpu.VMEM((T, W_BRANCH), F32),
        pltpu.VMEM((N_QUADS, QUAD, QUAD), F32),
        pltpu.VMEM((T + SUBLANES, W_BRANCH), F32),
        pltpu.VMEM((T, W_BRANCH), F32),
        pltpu.VMEM((1, W_BRANCH), F32),
        pltpu.VMEM((2, T, HIST + T), F32),
        pltpu.VMEM((2, T, HIST + T), BF16),
        pltpu.VMEM((2, T, LANES), F32),
    ]
    return pl.pallas_call(
        functools.partial(_mixer_kernel, T=T, prompt=prompt, n_tiles=n_tiles, kv_first=kv_first,
                          n_alias=len(aliases)),
        grid=(B, n_tiles),
        in_specs=in_specs,
        out_specs=out_specs,
        out_shape=out_shape,
        scratch_shapes=scratch,
        input_output_aliases=aliases,
        compiler_params=pltpu.CompilerParams(
            dimension_semantics=("arbitrary", "arbitrary"), vmem_limit_bytes=VMEM_LIMIT_BYTES),
        name="mixer_prompt" if prompt else "mixer_sample",
    )(*args)


def _merge_kernel(x_ref, y_ref, p_ref, wg0_ref, wg1_ref, wg2_ref, wb_ref, wo_ref, wpg_ref, wpe_ref, lng_ref,
                  lnb_ref, o_ref, *, alpha):
    x = x_ref[...]
    xb = x.astype(BF16)
    merged = None
    for b, wg_ref in enumerate((wg0_ref, wg1_ref, wg2_ref)):
        gate = _sigmoid(jnp.dot(xb, wg_ref[...], preferred_element_type=F32))
        term = gate * jnp.dot(y_ref[:, b * W_BRANCH:(b + 1) * W_BRANCH], wb_ref[b], preferred_element_type=F32)
        merged = term if merged is None else merged + term
    r = alpha * x + jnp.dot(merged.astype(BF16), wo_ref[...], preferred_element_type=F32)
    pgate = _sigmoid(jnp.dot(r.astype(BF16), wpg_ref[...], preferred_element_type=F32))
    r = r + pgate * jnp.dot(p_ref[...].astype(BF16), wpe_ref[...], preferred_element_type=F32)
    mu = jnp.mean(r, axis=-1, keepdims=True)
    dev = r - mu
    var = jnp.mean(dev * dev, axis=-1, keepdims=True)
    o_ref[...] = dev * lax.rsqrt(var + LN_EPS) * lng_ref[...] + lnb_ref[...]


def _merge_call(layer, x2, y2, p3, w_in_b, merge_w, *, alpha):
    n, d = x2.shape
    tm = min(MERGE_TILE, n)
    gate_block0 = N_MIX_COLS // d
    return pl.pallas_call(
        functools.partial(_merge_kernel, alpha=alpha),
        grid=(n // tm,),
        in_specs=[
            pl.BlockSpec((tm, d), lambda i: (i, 0)),
            pl.BlockSpec((tm, y2.shape[1]), lambda i: (i, 0)),
            pl.BlockSpec((None, tm, p3.shape[2]), lambda i: (layer, i, 0)),
        ] + [_layer_spec(w_in_b, layer, block=(d, d), index=(0, gate_block0 + b)) for b in range(3)]
          + [_layer_spec(a, layer) for a in merge_w],
        out_specs=pl.BlockSpec((tm, d), lambda i: (i, 0)),
        out_shape=jax.ShapeDtypeStruct((n, d), F32),
        compiler_params=pltpu.CompilerParams(
            dimension_semantics=("arbitrary",), vmem_limit_bytes=VMEM_LIMIT_BYTES),
        name="merge",
    )(x2, y2, p3, w_in_b, w_in_b, w_in_b, *merge_w)


def _tables(pos, blk):
    half = HEAD_DIM // 2
    inv = ROPE_BASE ** (-np.arange(half, dtype=np.float64) / half)
    ang = np.asarray(pos, np.float64)[:, None] * inv[None, :]
    c, s = np.cos(ang), np.sin(ang)
    cos = np.tile(np.concatenate([c, c], axis=-1), (1, N_HEADS))
    sin = np.tile(np.concatenate([-s, s], axis=-1), (1, N_HEADS))

    log_g = np.log1p(-np.exp2(-5.0 - np.arange(N_HEADS, dtype=np.float64)))
    i = np.arange(blk, dtype=np.float64)
    diff = i[:, None] - i[None, :]
    decay = np.where(diff >= 0, np.exp(log_g[:, None, None] * np.maximum(diff, 0.0)), 0.0)
    zeta = np.exp(log_g[:, None] * (blk - 1.0 - i)[None, :])
    xi = np.exp(log_g[:, None] * (i + 1.0)[None, :])
    g_blk = np.exp(log_g * blk)
    xi_full = np.repeat(xi.T, HEAD_DIM, axis=1)
    zeta_full = np.repeat(zeta.T, HEAD_DIM, axis=1)
    q = np.arange(QUAD)
    bmask = (q[:, None] // HEAD_DIM == q[None, :] // HEAD_DIM).astype(np.float64)
    gmat = np.stack([bmask * np.repeat(g_blk[4 * g:4 * g + 4], HEAD_DIM)[None, :] for g in range(N_QUADS)])
    lane = np.arange(W_BRANCH)
    seg = (lane[:, None] // HEAD_DIM == lane[None, :] // HEAD_DIM) / HEAD_DIM
    f32 = lambda a: jnp.asarray(a, F32)
    return (f32(cos), f32(sin), f32(decay), f32(xi_full), f32(zeta_full), f32(gmat), f32(bmask),
            jnp.asarray(seg, BF16))


def _block_diag(w):
    depth, n, bi, bj = w.shape
    tiled = jnp.tile(w.reshape(depth, n * bi, bj), (1, 1, n))
    r = np.arange(n * bi)[:, None] // bi
    c = np.arange(n * bj)[None, :] // bj
    return jnp.where(jnp.asarray(r == c)[None], tiled, 0.0)


def kernel(x_prompt, x_sample, p_prompt, p_sample, cache_k_a, cache_v_a, state_ret, state_conv, state_lru, w_in, rel_table, gn_gain, conv_w, conv_b, w_gate_a, b_gate_a, w_gate_x, b_gate_x, lru_lambda, w_branch, w_out, ln_gain, ln_bias, w_ple, w_ple_gate):
    depth = w_in.shape[0]
    bp, sp, d_model = x_prompt.shape
    bs, ss, _ = x_sample.shape
    assert cache_k_a.shape[2] == HIST and w_in.shape[2] == N_MIX_COLS + 3 * d_model
    assert N_MIX_COLS % d_model == 0
    tp = min(PROMPT_TILE, sp)
    assert sp % tp == 0 and tp % LANES == 0 and HIST % tp == 0 and ss % SUBLANES == 0
    alpha = float((2 * depth) ** 0.25)

    w_in_b = w_in.astype(BF16)
    row = lambda a: a.reshape(depth, 1, a.shape[-1])
    layer_w = (row(gn_gain), conv_w, row(conv_b), _block_diag(w_gate_a).astype(BF16), row(b_gate_a),
               _block_diag(w_gate_x).astype(BF16), row(b_gate_x), row(lru_lambda))
    merge_w = (w_branch.astype(BF16), w_out.astype(BF16), w_ple_gate.astype(BF16), w_ple.astype(BF16),
               row(ln_gain), row(ln_bias))

    bias_p, bias_s = _expand_bias(rel_table, tp, ss)
    tables_p = _tables(np.arange(sp), tp)
    tables_s = _tables(PAST_LEN + np.arange(ss), ss)
    cache = (cache_k_a.reshape(depth, bs, HIST, W_BRANCH), cache_v_a.reshape(depth, bs, HIST, W_BRANCH),
             state_ret, state_conv, state_lru.reshape(depth, bs, 1, W_BRANCH))
    pp = p_prompt.reshape(depth, bp * sp, -1)
    ps = p_sample.reshape(depth, bs * ss, -1)

    hp, hs = x_prompt, x_sample
    outs_p = outs_s = None
    for l in range(depth):
        y, *outs_p = _mixer_call(l, hp, tables_p, w_in_b, bias_p, layer_w, None, outs_p, T=tp, prompt=True)
        hp = _merge_call(l, hp.reshape(bp * sp, d_model), y.reshape(bp * sp, -1), pp, w_in_b, merge_w,
                         alpha=alpha).reshape(bp, sp, d_model)
        y, *outs_s = _mixer_call(l, hs, tables_s, w_in_b, bias_s, layer_w, cache, outs_s, T=ss, prompt=False)
        hs = _merge_call(l, hs.reshape(bs * ss, d_model), y.reshape(bs * ss, -1), ps, w_in_b, merge_w,
                         alpha=alpha).reshape(bs, ss, d_model)

    k_p, v_p, r_p, c_p, s_p = outs_p
    k_s, v_s, r_s, c_s, s_s = outs_s
    heads = lambda a: a.reshape(a.shape[:-1] + (N_HEADS, HEAD_DIM))
    return (hp, hs, heads(k_p), heads(v_p), heads(k_s), heads(v_s), r_p, r_s, c_p, c_s,
            s_p.reshape(depth, bp, W_BRANCH), s_s.reshape(depth, bs, W_BRANCH))
```

```python
import functools

import numpy as np

import jax
import jax.numpy as jnp
from jax import lax
from jax.experimental import pallas as pl
from jax.experimental.pallas import tpu as pltpu

F32 = jnp.float32
BF16 = jnp.bfloat16

CHUNK = 64
N_LEFT_CHUNKS = 8
HIST = N_LEFT_CHUNKS * CHUNK
HEAD_DIM = 64
N_HEADS = 8
W_BRANCH = N_HEADS * HEAD_DIM
N_PAIRS = N_HEADS // 2
LANES = 128
SUBLANES = 8
QUAD = 4 * HEAD_DIM
N_QUADS = N_HEADS // 4
REL_CLIP = 128
CONV_W = 4
LRU_C = 8.0
ROPE_BASE = 10000.0
LN_EPS = 1e-5
NEG_INF = -1e30
LOG2E = 1.4426950408889634
PAST_LEN = 1024
N_MIX_COLS = 10 * W_BRANCH
PROMPT_TILE = 256
MERGE_TILE = 512
MERGE_ROWS = 256
SOFTMAX_ROWS = 16
SQRT_FLOOR = 1e-30
VMEM_LIMIT_BYTES = 56 * 1024 * 1024

_NT = (((1,), (1,)), ((), ()))
_TN = (((0,), (0,)), ((), ()))


def _const_spec(shape):
    zeros = (0,) * len(shape)
    return pl.BlockSpec(shape, lambda *_: zeros, pipeline_mode=pl.Buffered(1))


def _layer_spec(arr, layer, block=None, index=None):
    block = tuple(arr.shape[1:]) if block is None else tuple(block)
    index = (0,) * len(block) if index is None else tuple(index)
    return pl.BlockSpec((None,) + block, lambda *_: (layer,) + index, pipeline_mode=pl.Buffered(1))


def _sigmoid(x):
    return 0.5 * jnp.tanh(0.5 * x) + 0.5


def _silu(x):
    half = 0.5 * x
    return half * jnp.tanh(half) + half


def _bias_kernel(tab_ref, bp_ref, bs_ref, *, tq, ts):
    h = pl.program_id(0)

    def toeplitz_block(rows, i0, j0):
        i = lax.broadcasted_iota(jnp.int32, (rows, LANES), 0) + i0
        j = lax.broadcasted_iota(jnp.int32, (rows, LANES), 1) + j0
        idx = jnp.clip(HIST + i - j, -REL_CLIP, REL_CLIP) + REL_CLIP
        d_lo = HIST + i0 - (j0 + LANES - 1)
        d_hi = HIST + i0 + rows - 1 - j0
        k_lo = min(max(d_lo, -REL_CLIP), REL_CLIP) + REL_CLIP
        k_hi = min(max(d_hi, -REL_CLIP), REL_CLIP) + REL_CLIP

        def body(k, acc):
            return jnp.where(idx == k, tab_ref[h, k], acc)

        blk = lax.fori_loop(k_lo, k_hi + 1, body, jnp.zeros((rows, LANES), F32), unroll=8)
        return blk * LOG2E, i, j

    for ib in range(tq // LANES):
        for jb in range((HIST + tq) // LANES):
            blk, i, j = toeplitz_block(LANES, ib * LANES, jb * LANES)
            qc = i // CHUNK
            kc = j // CHUNK
            vis = (kc >= qc) & (kc <= qc + N_LEFT_CHUNKS)
            bp_ref[0, ib * LANES:(ib + 1) * LANES, jb * LANES:(jb + 1) * LANES] = jnp.where(vis, blk, NEG_INF)
    for jb in range(bs_ref.shape[2] // LANES):
        blk, _, _ = toeplitz_block(ts, 0, jb * LANES)
        bs_ref[0, :, jb * LANES:(jb + 1) * LANES] = blk


def _expand_bias(rel_table, tq, ts):
    depth, nh, nrel = rel_table.shape
    ws = -(-(HIST + ts) // LANES) * LANES
    bp, bs = pl.pallas_call(
        functools.partial(_bias_kernel, tq=tq, ts=ts),
        grid=(depth * nh,),
        in_specs=[pl.BlockSpec(memory_space=pltpu.SMEM)],
        out_specs=[pl.BlockSpec((1, tq, HIST + tq), lambda g: (g, 0, 0)),
                   pl.BlockSpec((1, ts, ws), lambda g: (g, 0, 0))],
        out_shape=[jax.ShapeDtypeStruct((depth * nh, tq, HIST + tq), F32),
                   jax.ShapeDtypeStruct((depth * nh, ts, ws), F32)],
        name="rel_bias",
    )(rel_table.reshape(depth * nh, nrel))
    return bp.reshape(depth, nh, tq, HIST + tq), bs.reshape(depth, nh, ts, ws)


def _mixer_kernel(*refs, T, prompt, n_tiles, n_alias):
    n_in = 19 if prompt else 24
    (x_ref, cos_ref, sin_ref, w_ref, bias_ref, dec_ref, xi_ref, zeta_ref, gmat_ref, bmask_ref, seg_ref, gn_ref,
     cw_ref, cb_ref, wga_ref, bga_ref, wgx_ref, bgx_ref, lam_ref) = refs[:19]
    if not prompt:
        kc_ref, vc_ref, s0_ref, c0_ref, h0_ref = refs[19:24]
    refs = refs[n_in + n_alias:]
    (y_ref, ko_ref, vo_ref, ret_ref, conv_ref, lru_ref,
     xb_s, qa_s, kown_s, vown_s, khist_s, vhist_s, ya_s, yb_s, st_s, xbuf, hseq_s, h_s,
     sc_s, pr_s, rl_s) = refs

    t = pl.program_id(1)
    tail = CONV_W - 1
    lane = lax.broadcasted_iota(jnp.int32, (T, LANES), 1)
    first_head = lane < HEAD_DIM

    def split_heads(pair):
        zero = jnp.zeros_like(pair)
        m = first_head if pair.shape[0] == T else lax.broadcasted_iota(jnp.int32, pair.shape, 1) < HEAD_DIM
        return jnp.where(m, pair, zero), jnp.where(m, zero, pair)

    if prompt:
        @pl.when(t == 0)
        def _():
            khist_s[...] = jnp.zeros_like(khist_s)
            vhist_s[...] = jnp.zeros_like(vhist_s)
            st_s[...] = jnp.zeros_like(st_s)
            xbuf[0:SUBLANES, :] = jnp.zeros((SUBLANES, W_BRANCH), F32)
            h_s[...] = jnp.zeros_like(h_s)
    else:
        kc = kc_ref[0].astype(BF16)
        vc = vc_ref[0].astype(BF16)
        for hp in range(N_PAIRS):
            sl = slice(hp * LANES, (hp + 1) * LANES)
            khist_s[2 * hp], khist_s[2 * hp + 1] = split_heads(kc[:, sl])
            vhist_s[2 * hp], vhist_s[2 * hp + 1] = split_heads(vc[:, sl])
        st_s[...] = jnp.zeros_like(st_s)
        for h in range(N_HEADS):
            d0 = (h % 4) * HEAD_DIM
            st_s[h // 4, d0:d0 + HEAD_DIM, d0:d0 + HEAD_DIM] = s0_ref[0, h]
        xbuf[0:SUBLANES, :] = jnp.zeros((SUBLANES, W_BRANCH), F32)
        xbuf[SUBLANES - tail:SUBLANES, :] = c0_ref[0]
        h_s[...] = h0_ref[0]

    xb_s[...] = x_ref[0].astype(BF16)

    def proj(c):
        cols = slice(c * W_BRANCH, (c + 1) * W_BRANCH)
        return jnp.dot(xb_s[...], w_ref[:, cols], preferred_element_type=F32)

    def own_keys(c, out_ref, own_s):
        full = proj(c)
        out_ref[0] = full
        half = full.astype(BF16)
        for hp in range(N_PAIRS):
            sl = slice(hp * LANES, (hp + 1) * LANES)
            own_s[2 * hp], own_s[2 * hp + 1] = split_heads(half[:, sl])

    xr = proj(8)
    xbuf[SUBLANES:SUBLANES + T, :] = xr
    xc = cb_ref[...] + xr * cw_ref[tail:CONV_W, :]
    for j in range(tail):
        xc = xc + xbuf[SUBLANES - tail + j:SUBLANES - tail + j + T, :] * cw_ref[j:j + 1, :]
    xcb = xc.astype(BF16)
    gate_a = jnp.dot(xcb, wga_ref[...], preferred_element_type=F32)
    gate_x = jnp.dot(xcb, wgx_ref[...], preferred_element_type=F32)
    qa_s[...] = (proj(0) * (HEAD_DIM ** -0.5 * LOG2E)).astype(BF16)
    r = _sigmoid(gate_a + bga_ref[...])
    ig = _sigmoid(gate_x + bgx_ref[...])
    log_a = (-LRU_C) * r * jax.nn.softplus(-lam_ref[...])
    a_t = jnp.exp(log_a)
    gap = 1.0 - a_t * a_t
    b_t = gap * lax.rsqrt(jnp.maximum(gap, SQRT_FLOOR)) * ig * xc
    n_groups = T // SUBLANES
    a3 = a_t.reshape(n_groups, SUBLANES, W_BRANCH)
    b3 = b_t.reshape(n_groups, SUBLANES, W_BRANCH)
    sub = lax.broadcasted_iota(jnp.int32, (n_groups, SUBLANES, W_BRANCH), 1)
    between_steps = {1: functools.partial(own_keys, 1, ko_ref, kown_s),
                     2: functools.partial(own_keys, 2, vo_ref, vown_s)}
    step = 1
    while step < SUBLANES:
        if step in between_steps:
            between_steps[step]()
        keep = sub >= step
        a_prev = jnp.where(keep, pltpu.roll(a3, step, 1), 1.0)
        b_prev = jnp.where(keep, pltpu.roll(b3, step, 1), 0.0)
        b3 = a3 * b_prev + b3
        a3 = a3 * a_prev
        step *= 2
    gated_c = _silu(proj(9))
    carry = h_s[...]
    for g in range(n_groups):
        hg = a3[g] * carry + b3[g]
        hseq_s[g * SUBLANES:(g + 1) * SUBLANES, :] = hg
        carry = hg[SUBLANES - 1:SUBLANES, :]
    h_s[...] = carry
    y_ref[0, :, 2 * W_BRANCH:3 * W_BRANCH] = (hseq_s[...] * gated_c).astype(BF16)
    xbuf[0:SUBLANES, :] = xbuf[T:T + SUBLANES, :]


    def attend(mask_start):
        if mask_start:
            col = lax.broadcasted_iota(jnp.int32, (1, HIST), 1)
            start_mask = jnp.where(t * T - HIST + col >= 0, 0.0, NEG_INF).astype(F32)

        def scores(h):
            sl = slice((h // 2) * LANES, (h // 2 + 1) * LANES)
            q_p = qa_s[:, sl]
            s_h = lax.dot_general(q_p, khist_s[h], _NT, preferred_element_type=F32) + bias_ref[h, :, 0:HIST]
            if mask_start:
                s_h = s_h + start_mask
            sc_s[h % 2, :, 0:HIST] = s_h
            sc_s[h % 2, :, HIST:HIST + T] = (lax.dot_general(q_p, kown_s[h], _NT, preferred_element_type=F32)
                                             + bias_ref[h, :, HIST:HIST + T])

        def softmax(h):
            for r0 in range(0, T, SOFTMAX_ROWS):
                rows = slice(r0, r0 + SOFTMAX_ROWS)
                s = sc_s[h % 2, rows, :]
                p = jnp.exp2(s - jnp.max(s, axis=1, keepdims=True))
                pr_s[h % 2, rows, :] = p.astype(BF16)
                rl_s[h % 2, rows, :] = jnp.broadcast_to(1.0 / jnp.sum(p, axis=1, keepdims=True),
                                                        (SOFTMAX_ROWS, LANES))

        def weighted_values(h):
            sl = slice((h // 2) * LANES, (h // 2 + 1) * LANES)
            o = (jnp.dot(pr_s[h % 2, :, 0:HIST], vhist_s[h], preferred_element_type=F32)
                 + jnp.dot(pr_s[h % 2, :, HIST:HIST + T], vown_s[h], preferred_element_type=F32))
            o = o * rl_s[h % 2]
            ya_s[:, sl] = o if h % 2 == 0 else ya_s[:, sl] + o

        scores(0)
        for h in range(N_HEADS):
            if h + 1 < N_HEADS:
                scores(h + 1)
            softmax(h)
            weighted_values(h)

    if prompt:
        pl.when(t < HIST // T)(functools.partial(attend, True))
        pl.when(t >= HIST // T)(functools.partial(attend, False))
        for h in range(N_HEADS):
            if T < HIST:
                khist_s[h, 0:HIST - T] = khist_s[h, T:HIST]
                vhist_s[h, 0:HIST - T] = vhist_s[h, T:HIST]
            khist_s[h, HIST - T:HIST] = kown_s[h]
            vhist_s[h, HIST - T:HIST] = vown_s[h]
    else:
        attend(False)

    y_ref[0, :, 0:W_BRANCH] = (ya_s[...] * _silu(proj(3))).astype(BF16)

    cos = cos_ref[...]
    sin = sin_ref[...]
    lane_w = lax.broadcasted_iota(jnp.int32, (T, W_BRANCH), 1)
    low_half = (lane_w % HEAD_DIM) < (HEAD_DIM // 2)

    def rope(v):
        partner = jnp.where(low_half, pltpu.roll(v, W_BRANCH - HEAD_DIM // 2, 1), pltpu.roll(v, HEAD_DIM // 2, 1))
        return v * cos + partner * sin

    qr = rope(proj(4))
    kr = rope(proj(5)) * (HEAD_DIM ** -0.5)
    q_in = qr.astype(BF16)
    q_x = (qr * xi_ref[...]).astype(BF16)
    k_in = kr.astype(BF16)
    k_z = (kr * zeta_ref[...]).astype(BF16)
    v_b = proj(6).astype(BF16)
    gated_b = _silu(proj(7))
    for g in range(N_QUADS):
        gs = slice(g * QUAD, (g + 1) * QUAD)
        state = st_s[g]
        yb_s[:, gs] = jnp.dot(q_x[:, gs], state.astype(BF16), preferred_element_type=F32)
        outer = lax.dot_general(k_z[:, gs], v_b[:, gs], _TN, preferred_element_type=F32)
        st_s[g] = gmat_ref[g] * state + bmask_ref[...] * outer
    for hp in range(N_PAIRS):
        sl = slice(hp * LANES, (hp + 1) * LANES)
        k_heads = split_heads(k_in[:, sl])
        v_heads = split_heads(v_b[:, sl])
        acc = yb_s[:, sl]
        for e in range(2):
            h = 2 * hp + e
            qk = lax.dot_general(q_in[:, sl], k_heads[e], _NT, preferred_element_type=F32)
            inner = (qk * dec_ref[h]).astype(BF16)
            acc = acc + jnp.dot(inner, v_heads[e], preferred_element_type=F32)
        yb_s[:, sl] = acc

    def seg_mean(a):
        return jnp.dot(a.astype(BF16), seg_ref[...], preferred_element_type=F32)

    yb = yb_s[...]
    dev = yb - seg_mean(yb)
    var = seg_mean(dev * dev)
    ybn = dev * lax.rsqrt(var + LN_EPS) * gn_ref[...]
    y_ref[0, :, W_BRANCH:2 * W_BRANCH] = (ybn * gated_b).astype(BF16)

    def write_state():
        for h in range(N_HEADS):
            d0 = (h % 4) * HEAD_DIM
            ret_ref[0, h] = st_s[h // 4, d0:d0 + HEAD_DIM, d0:d0 + HEAD_DIM]
        conv_ref[0] = xbuf[SUBLANES - tail:SUBLANES, :]
        lru_ref[0] = h_s[...]

    if prompt:
        pl.when(t == n_tiles - 1)(write_state)
    else:
        write_state()


def _mixer_call(layer, x, tables, w_in_b, bias, layer_w, cache, prev, *, T, prompt):
    B, S, D = x.shape
    depth = w_in_b.shape[0]
    n_tiles = S // T
    kv_rows = min(HIST, S)
    kv_first = n_tiles - kv_rows // T
    cos, sin, dec, xi, zeta, gmat, bmask, seg = tables
    gn, cw, cb, wga, bga, wgx, bgx, lam = layer_w

    in_specs = [
        pl.BlockSpec((1, T, D), lambda b, t: (b, t, 0)),
        pl.BlockSpec((T, W_BRANCH), lambda b, t: (t, 0)),
        pl.BlockSpec((T, W_BRANCH), lambda b, t: (t, 0)),
        _layer_spec(w_in_b, layer, block=(D, N_MIX_COLS)),
        _layer_spec(bias, layer),
        _const_spec(dec.shape), _const_spec(xi.shape), _const_spec(zeta.shape),
        _const_spec(gmat.shape), _const_spec(bmask.shape), _const_spec(seg.shape),
    ] + [_layer_spec(a, layer) for a in layer_w]
    args = [x, cos, sin, w_in_b, bias, dec, xi, zeta, gmat, bmask, seg, gn, cw, cb, wga, bga, wgx, bgx, lam]
    if not prompt:
        for a in cache:
            blk = (1,) + tuple(a.shape[2:])
            zeros = (0,) * (len(blk) - 1)
            in_specs.append(pl.BlockSpec((None,) + blk, lambda b, t, zeros=zeros: (layer, b) + zeros))
        args += list(cache)
    aliases = {}
    if prev is not None:
        for k, a in enumerate(prev):
            in_specs.append(pl.BlockSpec(memory_space=pl.ANY))
            aliases[len(args)] = k + 1
            args.append(a)

    def kv_map(b, t):
        return (layer, b, jnp.maximum(t - kv_first, 0), 0)

    out_specs = [
        pl.BlockSpec((1, T, 3 * W_BRANCH), lambda b, t: (b, t, 0)),
        pl.BlockSpec((None, 1, T, W_BRANCH), kv_map),
        pl.BlockSpec((None, 1, T, W_BRANCH), kv_map),
        pl.BlockSpec((None, 1, N_HEADS, HEAD_DIM, HEAD_DIM), lambda b, t: (layer, b, 0, 0, 0)),
        pl.BlockSpec((None, 1, CONV_W - 1, W_BRANCH), lambda b, t: (layer, b, 0, 0)),
        pl.BlockSpec((None, 1, 1, W_BRANCH), lambda b, t: (layer, b, 0, 0)),
    ]
    out_shape = [
        jax.ShapeDtypeStruct((B, S, 3 * W_BRANCH), BF16),
        jax.ShapeDtypeStruct((depth, B, kv_rows, W_BRANCH), F32),
        jax.ShapeDtypeStruct((depth, B, kv_rows, W_BRANCH), F32),
        jax.ShapeDtypeStruct((depth, B, N_HEADS, HEAD_DIM, HEAD_DIM), F32),
        jax.ShapeDtypeStruct((depth, B, CONV_W - 1, W_BRANCH), F32),
        jax.ShapeDtypeStruct((depth, B, 1, W_BRANCH), F32),
    ]
    scratch = [
        pltpu.VMEM((T, D), BF16),
        pltpu.VMEM((T, W_BRANCH), BF16),
        pltpu.VMEM((N_HEADS, T, LANES), BF16),
        pltpu.VMEM((N_HEADS, T, LANES), BF16),
        pltpu.VMEM((N_HEADS, HIST, LANES), BF16),
        pltpu.VMEM((N_HEADS, HIST, LANES), BF16),
        pltpu.VMEM((T, W_BRANCH), F32),
        pltpu.VMEM((T, W_BRANCH), F32),
        pltpu.VMEM((N_QUADS, QUAD, QUAD), F32),
        pltpu.VMEM((T + SUBLANES, W_BRANCH), F32),
        pltpu.VMEM((T, W_BRANCH), F32),
        pltpu.VMEM((1, W_BRANCH), F32),
        pltpu.VMEM((2, T, HIST + T), F32),
        pltpu.VMEM((2, T, HIST + T), BF16),
        pltpu.VMEM((2, T, LANES), F32),
    ]
    return pl.pallas_call(
        functools.partial(_mixer_kernel, T=T, prompt=prompt, n_tiles=n_tiles, n_alias=len(aliases)),
        grid=(B, n_tiles),
        in_specs=in_specs,
        out_specs=out_specs,
        out_shape=out_shape,
        scratch_shapes=scratch,
        input_output_aliases=aliases,
        compiler_params=pltpu.CompilerParams(
            dimension_semantics=("arbitrary", "arbitrary"), vmem_limit_bytes=VMEM_LIMIT_BYTES,
            ),
        name="mixer_prompt" if prompt else "mixer_sample",
    )(*args)


def _merge_kernel(x_ref, y_ref, p_ref, wg0_ref, wg1_ref, wg2_ref, wb_ref, wo_ref, wpg_ref, wpe_ref, lng_ref,
                  lnb_ref, o_ref, m_s, r_s, *, alpha, rows_per_pass):
    passes = [slice(r0, r0 + rows_per_pass) for r0 in range(0, x_ref.shape[0], rows_per_pass)]

    def branch_merge(rows):
        xb = x_ref[rows, :].astype(BF16)
        merged = None
        for b, wg_ref in enumerate((wg0_ref, wg1_ref, wg2_ref)):
            gate = _sigmoid(jnp.dot(xb, wg_ref[...], preferred_element_type=F32))
            term = gate * jnp.dot(y_ref[rows, b * W_BRANCH:(b + 1) * W_BRANCH], wb_ref[b],
                                  preferred_element_type=F32)
            merged = term if merged is None else merged + term
        m_s[rows, :] = merged.astype(BF16)

    def residual(rows):
        r = alpha * x_ref[rows, :] + jnp.dot(m_s[rows, :], wo_ref[...], preferred_element_type=F32)
        pgate = _sigmoid(jnp.dot(r.astype(BF16), wpg_ref[...], preferred_element_type=F32))
        r_s[rows, :] = r + pgate * jnp.dot(p_ref[rows, :].astype(BF16), wpe_ref[...], preferred_element_type=F32)

    def layer_norm(rows):
        r = r_s[rows, :]
        mu = jnp.mean(r, axis=-1, keepdims=True)
        dev = r - mu
        var = jnp.mean(dev * dev, axis=-1, keepdims=True)
        o_ref[rows, :] = dev * lax.rsqrt(var + LN_EPS) * lng_ref[...] + lnb_ref[...]

    branch_merge(passes[0])
    residual(passes[0])
    for prev, rows in zip(passes[:-1], passes[1:]):
        branch_merge(rows)
        layer_norm(prev)
        residual(rows)
    layer_norm(passes[-1])


def _merge_call(layer, x2, y2, p3, w_in_b, merge_w, *, alpha):
    n, d = x2.shape
    tm = min(MERGE_TILE, n)
    rows_per_pass = min(MERGE_ROWS, tm)
    gate_block0 = N_MIX_COLS // d
    return pl.pallas_call(
        functools.partial(_merge_kernel, alpha=alpha, rows_per_pass=rows_per_pass),
        grid=(n // tm,),
        scratch_shapes=[pltpu.VMEM((tm, d), BF16),
                        pltpu.VMEM((tm, d), F32)],
        in_specs=[
            pl.BlockSpec((tm, d), lambda i: (i, 0)),
            pl.BlockSpec((tm, y2.shape[1]), lambda i: (i, 0)),
            pl.BlockSpec((None, tm, p3.shape[2]), lambda i: (layer, i, 0)),
        ] + [_layer_spec(w_in_b, layer, block=(d, d), index=(0, gate_block0 + b)) for b in range(3)]
          + [_layer_spec(a, layer) for a in merge_w],
        out_specs=pl.BlockSpec((tm, d), lambda i: (i, 0)),
        out_shape=jax.ShapeDtypeStruct((n, d), F32),
        compiler_params=pltpu.CompilerParams(
            dimension_semantics=("arbitrary",), vmem_limit_bytes=VMEM_LIMIT_BYTES),
        name="merge",
    )(x2, y2, p3, w_in_b, w_in_b, w_in_b, *merge_w)


def _tables(pos, blk):
    half = HEAD_DIM // 2
    inv = ROPE_BASE ** (-np.arange(half, dtype=np.float64) / half)
    ang = np.asarray(pos, np.float64)[:, None] * inv[None, :]
    c, s = np.cos(ang), np.sin(ang)
    cos = np.tile(np.concatenate([c, c], axis=-1), (1, N_HEADS))
    sin = np.tile(np.concatenate([-s, s], axis=-1), (1, N_HEADS))

    log_g = np.log1p(-np.exp2(-5.0 - np.arange(N_HEADS, dtype=np.float64)))
    i = np.arange(blk, dtype=np.float64)
    diff = i[:, None] - i[None, :]
    decay = np.where(diff >= 0, np.exp(log_g[:, None, None] * np.maximum(diff, 0.0)), 0.0)
    zeta = np.exp(log_g[:, None] * (blk - 1.0 - i)[None, :])
    xi = np.exp(log_g[:, None] * (i + 1.0)[None, :])
    g_blk = np.exp(log_g * blk)
    xi_full = np.repeat(xi.T, HEAD_DIM, axis=1)
    zeta_full = np.repeat(zeta.T, HEAD_DIM, axis=1)
    q = np.arange(QUAD)
    bmask = (q[:, None] // HEAD_DIM == q[None, :] // HEAD_DIM).astype(np.float64)
    gmat = np.stack([bmask * np.repeat(g_blk[4 * g:4 * g + 4], HEAD_DIM)[None, :] for g in range(N_QUADS)])
    lane = np.arange(W_BRANCH)
    seg = (lane[:, None] // HEAD_DIM == lane[None, :] // HEAD_DIM) / HEAD_DIM
    f32 = lambda a: jnp.asarray(a, F32)
    return (f32(cos), f32(sin), f32(decay), f32(xi_full), f32(zeta_full), f32(gmat), f32(bmask),
            jnp.asarray(seg, BF16))


def _block_diag(w):
    depth, n, bi, bj = w.shape
    tiled = jnp.tile(w.reshape(depth, n * bi, bj), (1, 1, n))
    r = np.arange(n * bi)[:, None] // bi
    c = np.arange(n * bj)[None, :] // bj
    return jnp.where(jnp.asarray(r == c)[None], tiled, 0.0)


def kernel(x_prompt, x_sample, p_prompt, p_sample, cache_k_a, cache_v_a, state_ret, state_conv, state_lru, w_in, rel_table, gn_gain, conv_w, conv_b, w_gate_a, b_gate_a, w_gate_x, b_gate_x, lru_lambda, w_branch, w_out, ln_gain, ln_bias, w_ple, w_ple_gate):
    depth = w_in.shape[0]
    bp, sp, d_model = x_prompt.shape
    bs, ss, _ = x_sample.shape
    assert cache_k_a.shape[2] == HIST and w_in.shape[2] == N_MIX_COLS + 3 * d_model
    assert N_MIX_COLS % d_model == 0
    tp = min(PROMPT_TILE, sp)
    assert sp % tp == 0 and tp % LANES == 0 and HIST % tp == 0 and ss % SUBLANES == 0
    alpha = float((2 * depth) ** 0.25)

    w_in_b = w_in.astype(BF16)
    row = lambda a: a.reshape(depth, 1, a.shape[-1])
    layer_w = (row(gn_gain), conv_w, row(conv_b), _block_diag(w_gate_a).astype(BF16), row(b_gate_a),
               _block_diag(w_gate_x).astype(BF16), row(b_gate_x), row(lru_lambda))
    merge_w = (w_branch.astype(BF16), w_out.astype(BF16), w_ple_gate.astype(BF16), w_ple.astype(BF16),
               row(ln_gain), row(ln_bias))

    bias_p, bias_s = _expand_bias(rel_table, tp, ss)
    tables_p = _tables(np.arange(sp), tp)
    tables_s = _tables(PAST_LEN + np.arange(ss), ss)
    cache = (cache_k_a.reshape(depth, bs, HIST, W_BRANCH), cache_v_a.reshape(depth, bs, HIST, W_BRANCH),
             state_ret, state_conv, state_lru.reshape(depth, bs, 1, W_BRANCH))
    pp = p_prompt.reshape(depth, bp * sp, -1)
    ps = p_sample.reshape(depth, bs * ss, -1)

    hp, hs = x_prompt, x_sample
    outs_p = outs_s = None
    for l in range(depth):
        y, *outs_p = _mixer_call(l, hp, tables_p, w_in_b, bias_p, layer_w, None, outs_p, T=tp, prompt=True)
        hp = _merge_call(l, hp.reshape(bp * sp, d_model), y.reshape(bp * sp, -1), pp, w_in_b, merge_w,
                         alpha=alpha).reshape(bp, sp, d_model)
        y, *outs_s = _mixer_call(l, hs, tables_s, w_in_b, bias_s, layer_w, cache, outs_s, T=ss, prompt=False)
        hs = _merge_call(l, hs.reshape(bs * ss, d_model), y.reshape(bs * ss, -1), ps, w_in_b, merge_w,
                         alpha=alpha).reshape(bs, ss, d_model)

    k_p, v_p, r_p, c_p, s_p = outs_p
    k_s, v_s, r_s, c_s, s_s = outs_s
    heads = lambda a: a.reshape(a.shape[:-1] + (N_HEADS, HEAD_DIM))
    return (hp, hs, heads(k_p), heads(v_p), heads(k_s), heads(v_s), r_p, r_s, c_p, c_s,
            s_p.reshape(depth, bp, W_BRANCH), s_s.reshape(depth, bs, W_BRANCH))
```

```python
import functools

import numpy as np

import jax
import jax.numpy as jnp
from jax import lax
from jax.experimental import pallas as pl
from jax.experimental.pallas import tpu as pltpu

F32 = jnp.float32
BF16 = jnp.bfloat16

CHUNK = 64
N_LEFT_CHUNKS = 8
HIST = N_LEFT_CHUNKS * CHUNK
HEAD_DIM = 64
N_HEADS = 8
W_BRANCH = N_HEADS * HEAD_DIM
N_PAIRS = N_HEADS // 2
LANES = 128
SUBLANES = 8
QUAD = 4 * HEAD_DIM
N_QUADS = N_HEADS // 4
REL_CLIP = 128
CONV_W = 4
LRU_C = 8.0
ROPE_BASE = 10000.0
LN_EPS = 1e-5
NEG_INF = -1e30
LOG2E = 1.4426950408889634
PAST_LEN = 1024
N_MIX_COLS = 10 * W_BRANCH
PROMPT_TILE = 256
MERGE_TILE = 1024
MERGE_ROWS = 256
SOFTMAX_ROWS = 16
SQRT_FLOOR = 1e-30
VMEM_LIMIT_BYTES = 56 * 1024 * 1024

_NT = (((1,), (1,)), ((), ()))
_TN = (((0,), (0,)), ((), ()))


def _const_spec(shape):
    zeros = (0,) * len(shape)
    return pl.BlockSpec(shape, lambda *_: zeros, pipeline_mode=pl.Buffered(1))


def _layer_spec(arr, layer, block=None, index=None):
    block = tuple(arr.shape[1:]) if block is None else tuple(block)
    index = (0,) * len(block) if index is None else tuple(index)
    return pl.BlockSpec((None,) + block, lambda *_: (layer,) + index, pipeline_mode=pl.Buffered(1))


def _sigmoid(x):
    return 0.5 * jnp.tanh(0.5 * x) + 0.5


def _silu(x):
    half = 0.5 * x
    return half * jnp.tanh(half) + half


def _bias_kernel(tab_ref, bp_ref, bs_ref, *, tq, ts):
    h = pl.program_id(0)

    def toeplitz_block(rows, i0, j0):
        i = lax.broadcasted_iota(jnp.int32, (rows, LANES), 0) + i0
        j = lax.broadcasted_iota(jnp.int32, (rows, LANES), 1) + j0
        idx = jnp.clip(HIST + i - j, -REL_CLIP, REL_CLIP) + REL_CLIP
        d_lo = HIST + i0 - (j0 + LANES - 1)
        d_hi = HIST + i0 + rows - 1 - j0
        k_lo = min(max(d_lo, -REL_CLIP), REL_CLIP) + REL_CLIP
        k_hi = min(max(d_hi, -REL_CLIP), REL_CLIP) + REL_CLIP

        def body(k, acc):
            return jnp.where(idx == k, tab_ref[h, k], acc)

        blk = lax.fori_loop(k_lo, k_hi + 1, body, jnp.zeros((rows, LANES), F32), unroll=8)
        return blk * LOG2E, i, j

    for ib in range(tq // LANES):
        for jb in range((HIST + tq) // LANES):
            blk, i, j = toeplitz_block(LANES, ib * LANES, jb * LANES)
            qc = i // CHUNK
            kc = j // CHUNK
            vis = (kc >= qc) & (kc <= qc + N_LEFT_CHUNKS)
            bp_ref[0, ib * LANES:(ib + 1) * LANES, jb * LANES:(jb + 1) * LANES] = jnp.where(vis, blk, NEG_INF)
    for jb in range(bs_ref.shape[2] // LANES):
        blk, _, _ = toeplitz_block(ts, 0, jb * LANES)
        bs_ref[0, :, jb * LANES:(jb + 1) * LANES] = blk


def _expand_bias(rel_table, tq, ts):
    depth, nh, nrel = rel_table.shape
    ws = -(-(HIST + ts) // LANES) * LANES
    bp, bs = pl.pallas_call(
        functools.partial(_bias_kernel, tq=tq, ts=ts),
        grid=(depth * nh,),
        in_specs=[pl.BlockSpec(memory_space=pltpu.SMEM)],
        out_specs=[pl.BlockSpec((1, tq, HIST + tq), lambda g: (g, 0, 0)),
                   pl.BlockSpec((1, ts, ws), lambda g: (g, 0, 0))],
        out_shape=[jax.ShapeDtypeStruct((depth * nh, tq, HIST + tq), F32),
                   jax.ShapeDtypeStruct((depth * nh, ts, ws), F32)],
        name="rel_bias",
    )(rel_table.reshape(depth * nh, nrel))
    return bp.reshape(depth, nh, tq, HIST + tq), bs.reshape(depth, nh, ts, ws)


def _mixer_kernel(*refs, T, prompt, n_tiles, n_alias):
    n_in = 19 if prompt else 24
    (x_ref, cos_ref, sin_ref, w_ref, bias_ref, dec_ref, xi_ref, zeta_ref, gmat_ref, bmask_ref, seg_ref, gn_ref,
     cw_ref, cb_ref, wga_ref, bga_ref, wgx_ref, bgx_ref, lam_ref) = refs[:19]
    if not prompt:
        kc_ref, vc_ref, s0_ref, c0_ref, h0_ref = refs[19:24]
    refs = refs[n_in + n_alias:]
    (y_ref, ko_ref, vo_ref, ret_ref, conv_ref, lru_ref,
     xb_s, qa_s, kown_s, vown_s, khist_s, vhist_s, ya_s, yb_s, st_s, xbuf, hseq_s, h_s,
     sc_s, pr_s, rl_s) = refs

    t = pl.program_id(1)
    tail = CONV_W - 1
    lane = lax.broadcasted_iota(jnp.int32, (T, LANES), 1)
    first_head = lane < HEAD_DIM

    def split_heads(pair):
        zero = jnp.zeros_like(pair)
        m = first_head if pair.shape[0] == T else lax.broadcasted_iota(jnp.int32, pair.shape, 1) < HEAD_DIM
        return jnp.where(m, pair, zero), jnp.where(m, zero, pair)

    if prompt:
        @pl.when(t == 0)
        def _():
            khist_s[...] = jnp.zeros_like(khist_s)
            vhist_s[...] = jnp.zeros_like(vhist_s)
            st_s[...] = jnp.zeros_like(st_s)
            xbuf[0:SUBLANES, :] = jnp.zeros((SUBLANES, W_BRANCH), F32)
            h_s[...] = jnp.zeros_like(h_s)
    else:
        kc = kc_ref[0].astype(BF16)
        vc = vc_ref[0].astype(BF16)
        for hp in range(N_PAIRS):
            sl = slice(hp * LANES, (hp + 1) * LANES)
            khist_s[2 * hp], khist_s[2 * hp + 1] = split_heads(kc[:, sl])
            vhist_s[2 * hp], vhist_s[2 * hp + 1] = split_heads(vc[:, sl])
        st_s[...] = jnp.zeros_like(st_s)
        for h in range(N_HEADS):
            d0 = (h % 4) * HEAD_DIM
            st_s[h // 4, d0:d0 + HEAD_DIM, d0:d0 + HEAD_DIM] = s0_ref[0, h]
        xbuf[0:SUBLANES, :] = jnp.zeros((SUBLANES, W_BRANCH), F32)
        xbuf[SUBLANES - tail:SUBLANES, :] = c0_ref[0]
        h_s[...] = h0_ref[0]

    xb_s[...] = x_ref[0].astype(BF16)

    def proj(c):
        cols = slice(c * W_BRANCH, (c + 1) * W_BRANCH)
        return jnp.dot(xb_s[...], w_ref[:, cols], preferred_element_type=F32)

    def own_keys(c, out_ref, own_s):
        full = proj(c)
        out_ref[0] = full
        half = full.astype(BF16)
        for hp in range(N_PAIRS):
            sl = slice(hp * LANES, (hp + 1) * LANES)
            own_s[2 * hp], own_s[2 * hp + 1] = split_heads(half[:, sl])

    xr = proj(8)
    xbuf[SUBLANES:SUBLANES + T, :] = xr
    xc = cb_ref[...] + xr * cw_ref[tail:CONV_W, :]
    for j in range(tail):
        xc = xc + xbuf[SUBLANES - tail + j:SUBLANES - tail + j + T, :] * cw_ref[j:j + 1, :]
    xcb = xc.astype(BF16)
    gate_a = jnp.dot(xcb, wga_ref[...], preferred_element_type=F32)
    gate_x = jnp.dot(xcb, wgx_ref[...], preferred_element_type=F32)
    qa_s[...] = (proj(0) * (HEAD_DIM ** -0.5 * LOG2E)).astype(BF16)
    r = _sigmoid(gate_a + bga_ref[...])
    ig = _sigmoid(gate_x + bgx_ref[...])
    log_a = (-LRU_C) * r * jax.nn.softplus(-lam_ref[...])
    a_t = jnp.exp(log_a)
    gap = 1.0 - a_t * a_t
    b_t = gap * lax.rsqrt(jnp.maximum(gap, SQRT_FLOOR)) * ig * xc
    n_groups = T // SUBLANES
    a3 = a_t.reshape(n_groups, SUBLANES, W_BRANCH)
    b3 = b_t.reshape(n_groups, SUBLANES, W_BRANCH)
    sub = lax.broadcasted_iota(jnp.int32, (n_groups, SUBLANES, W_BRANCH), 1)
    between_steps = {1: functools.partial(own_keys, 1, ko_ref, kown_s),
                     2: functools.partial(own_keys, 2, vo_ref, vown_s)}
    step = 1
    while step < SUBLANES:
        if step in between_steps:
            between_steps[step]()
        keep = sub >= step
        a_prev = jnp.where(keep, pltpu.roll(a3, step, 1), 1.0)
        b_prev = jnp.where(keep, pltpu.roll(b3, step, 1), 0.0)
        b3 = a3 * b_prev + b3
        a3 = a3 * a_prev
        step *= 2
    gated_c = _silu(proj(9))
    carry = h_s[...]
    for g in range(n_groups):
        hg = a3[g] * carry + b3[g]
        hseq_s[g * SUBLANES:(g + 1) * SUBLANES, :] = hg
        carry = hg[SUBLANES - 1:SUBLANES, :]
    h_s[...] = carry
    y_ref[0, :, 2 * W_BRANCH:3 * W_BRANCH] = (hseq_s[...] * gated_c).astype(BF16)
    xbuf[0:SUBLANES, :] = xbuf[T:T + SUBLANES, :]


    def attend(hist_rows):
        h0 = HIST - hist_rows
        own = slice(HIST, HIST + T)

        def scores(h):
            sl = slice((h // 2) * LANES, (h // 2 + 1) * LANES)
            q_p = qa_s[:, sl]
            if hist_rows:
                sc_s[h % 2, :, h0:HIST] = (lax.dot_general(q_p, khist_s[h, h0:HIST], _NT, preferred_element_type=F32)
                                           + bias_ref[h, :, h0:HIST])
            sc_s[h % 2, :, own] = (lax.dot_general(q_p, kown_s[h], _NT, preferred_element_type=F32)
                                   + bias_ref[h, :, own])

        def softmax(h):
            for r0 in range(0, T, SOFTMAX_ROWS):
                rows = slice(r0, r0 + SOFTMAX_ROWS)
                s = sc_s[h % 2, rows, h0:HIST + T]
                p = jnp.exp2(s - jnp.max(s, axis=1, keepdims=True))
                pr_s[h % 2, rows, h0:HIST + T] = p.astype(BF16)
                rl_s[h % 2, rows, :] = jnp.broadcast_to(1.0 / jnp.sum(p, axis=1, keepdims=True),
                                                        (SOFTMAX_ROWS, LANES))

        def weighted_values(h):
            sl = slice((h // 2) * LANES, (h // 2 + 1) * LANES)
            o = jnp.dot(pr_s[h % 2, :, own], vown_s[h], preferred_element_type=F32)
            if hist_rows:
                o = o + jnp.dot(pr_s[h % 2, :, h0:HIST], vhist_s[h, h0:HIST], preferred_element_type=F32)
            o = o * rl_s[h % 2]
            ya_s[:, sl] = o if h % 2 == 0 else ya_s[:, sl] + o

        scores(0)
        for h in range(N_HEADS):
            if h + 1 < N_HEADS:
                scores(h + 1)
            softmax(h)
            weighted_values(h)

    if prompt:
        for k in range(HIST // T):
            pl.when(t == k)(functools.partial(attend, k * T))
        pl.when(t >= HIST // T)(functools.partial(attend, HIST))
        for h in range(N_HEADS):
            if T < HIST:
                khist_s[h, 0:HIST - T] = khist_s[h, T:HIST]
                vhist_s[h, 0:HIST - T] = vhist_s[h, T:HIST]
            khist_s[h, HIST - T:HIST] = kown_s[h]
            vhist_s[h, HIST - T:HIST] = vown_s[h]
    else:
        attend(HIST)

    y_ref[0, :, 0:W_BRANCH] = (ya_s[...] * _silu(proj(3))).astype(BF16)

    cos = cos_ref[...]
    sin = sin_ref[...]
    lane_w = lax.broadcasted_iota(jnp.int32, (T, W_BRANCH), 1)
    low_half = (lane_w % HEAD_DIM) < (HEAD_DIM // 2)

    def rope(v):
        partner = jnp.where(low_half, pltpu.roll(v, W_BRANCH - HEAD_DIM // 2, 1), pltpu.roll(v, HEAD_DIM // 2, 1))
        return v * cos + partner * sin

    qr = rope(proj(4))
    kr = rope(proj(5)) * (HEAD_DIM ** -0.5)
    q_in = qr.astype(BF16)
    q_x = (qr * xi_ref[...]).astype(BF16)
    k_in = kr.astype(BF16)
    k_z = (kr * zeta_ref[...]).astype(BF16)
    v_b = proj(6).astype(BF16)
    gated_b = _silu(proj(7))
    for g in range(N_QUADS):
        gs = slice(g * QUAD, (g + 1) * QUAD)
        state = st_s[g]
        yb_s[:, gs] = jnp.dot(q_x[:, gs], state.astype(BF16), preferred_element_type=F32)
        outer = lax.dot_general(k_z[:, gs], v_b[:, gs], _TN, preferred_element_type=F32)
        st_s[g] = gmat_ref[g] * state + bmask_ref[...] * outer
    for hp in range(N_PAIRS):
        sl = slice(hp * LANES, (hp + 1) * LANES)
        k_heads = split_heads(k_in[:, sl])
        v_heads = split_heads(v_b[:, sl])
        acc = yb_s[:, sl]
        for e in range(2):
            h = 2 * hp + e
            qk = lax.dot_general(q_in[:, sl], k_heads[e], _NT, preferred_element_type=F32)
            inner = (qk * dec_ref[h]).astype(BF16)
            acc = acc + jnp.dot(inner, v_heads[e], preferred_element_type=F32)
        yb_s[:, sl] = acc

    def seg_mean(a):
        return jnp.dot(a.astype(BF16), seg_ref[...], preferred_element_type=F32)

    yb = yb_s[...]
    dev = yb - seg_mean(yb)
    var = seg_mean(dev * dev)
    ybn = dev * lax.rsqrt(var + LN_EPS) * gn_ref[...]
    y_ref[0, :, W_BRANCH:2 * W_BRANCH] = (ybn * gated_b).astype(BF16)

    def write_state():
        for h in range(N_HEADS):
            d0 = (h % 4) * HEAD_DIM
            ret_ref[0, h] = st_s[h // 4, d0:d0 + HEAD_DIM, d0:d0 + HEAD_DIM]
        conv_ref[0] = xbuf[SUBLANES - tail:SUBLANES, :]
        lru_ref[0] = h_s[...]

    if prompt:
        pl.when(t == n_tiles - 1)(write_state)
    else:
        write_state()


def _mixer_call(layer, x, tables, w_in_b, bias, layer_w, cache, prev, *, T, prompt):
    B, S, D = x.shape
    depth = w_in_b.shape[0]
    n_tiles = S // T
    kv_rows = min(HIST, S)
    kv_first = n_tiles - kv_rows // T
    cos, sin, dec, xi, zeta, gmat, bmask, seg = tables
    gn, cw, cb, wga, bga, wgx, bgx, lam = layer_w

    in_specs = [
        pl.BlockSpec((1, T, D), lambda b, t: (b, t, 0)),
        pl.BlockSpec((T, W_BRANCH), lambda b, t: (t, 0)),
        pl.BlockSpec((T, W_BRANCH), lambda b, t: (t, 0)),
        _layer_spec(w_in_b, layer, block=(D, N_MIX_COLS)),
        _layer_spec(bias, layer),
        _const_spec(dec.shape), _const_spec(xi.shape), _const_spec(zeta.shape),
        _const_spec(gmat.shape), _const_spec(bmask.shape), _const_spec(seg.shape),
    ] + [_layer_spec(a, layer) for a in layer_w]
    args = [x, cos, sin, w_in_b, bias, dec, xi, zeta, gmat, bmask, seg, gn, cw, cb, wga, bga, wgx, bgx, lam]
    if not prompt:
        for a in cache:
            blk = (1,) + tuple(a.shape[2:])
            zeros = (0,) * (len(blk) - 1)
            in_specs.append(pl.BlockSpec((None,) + blk, lambda b, t, zeros=zeros: (layer, b) + zeros))
        args += list(cache)
    aliases = {}
    if prev is not None:
        for k, a in enumerate(prev):
            in_specs.append(pl.BlockSpec(memory_space=pl.ANY))
            aliases[len(args)] = k + 1
            args.append(a)

    def kv_map(b, t):
        return (layer, b, jnp.maximum(t - kv_first, 0), 0)

    out_specs = [
        pl.BlockSpec((1, T, 3 * W_BRANCH), lambda b, t: (b, t, 0)),
        pl.BlockSpec((None, 1, T, W_BRANCH), kv_map),
        pl.BlockSpec((None, 1, T, W_BRANCH), kv_map),
        pl.BlockSpec((None, 1, N_HEADS, HEAD_DIM, HEAD_DIM), lambda b, t: (layer, b, 0, 0, 0)),
        pl.BlockSpec((None, 1, CONV_W - 1, W_BRANCH), lambda b, t: (layer, b, 0, 0)),
        pl.BlockSpec((None, 1, 1, W_BRANCH), lambda b, t: (layer, b, 0, 0)),
    ]
    out_shape = [
        jax.ShapeDtypeStruct((B, S, 3 * W_BRANCH), BF16),
        jax.ShapeDtypeStruct((depth, B, kv_rows, W_BRANCH), F32),
        jax.ShapeDtypeStruct((depth, B, kv_rows, W_BRANCH), F32),
        jax.ShapeDtypeStruct((depth, B, N_HEADS, HEAD_DIM, HEAD_DIM), F32),
        jax.ShapeDtypeStruct((depth, B, CONV_W - 1, W_BRANCH), F32),
        jax.ShapeDtypeStruct((depth, B, 1, W_BRANCH), F32),
    ]
    scratch = [
        pltpu.VMEM((T, D), BF16),
        pltpu.VMEM((T, W_BRANCH), BF16),
        pltpu.VMEM((N_HEADS, T, LANES), BF16),
        pltpu.VMEM((N_HEADS, T, LANES), BF16),
        pltpu.VMEM((N_HEADS, HIST, LANES), BF16),
        pltpu.VMEM((N_HEADS, HIST, LANES), BF16),
        pltpu.VMEM((T, W_BRANCH), F32),
        pltpu.VMEM((T, W_BRANCH), F32),
        pltpu.VMEM((N_QUADS, QUAD, QUAD), F32),
        pltpu.VMEM((T + SUBLANES, W_BRANCH), F32),
        pltpu.VMEM((T, W_BRANCH), F32),
        pltpu.VMEM((1, W_BRANCH), F32),
        pltpu.VMEM((2, T, HIST + T), F32),
        pltpu.VMEM((2, T, HIST + T), BF16),
        pltpu.VMEM((2, T, LANES), F32),
    ]
    return pl.pallas_call(
        functools.partial(_mixer_kernel, T=T, prompt=prompt, n_tiles=n_tiles, n_alias=len(aliases)),
        grid=(B, n_tiles),
        in_specs=in_specs,
        out_specs=out_specs,
        out_shape=out_shape,
        scratch_shapes=scratch,
        input_output_aliases=aliases,
        compiler_params=pltpu.CompilerParams(
            dimension_semantics=("arbitrary", "arbitrary"), vmem_limit_bytes=VMEM_LIMIT_BYTES,
            ),
        name="mixer_prompt" if prompt else "mixer_sample",
    )(*args)


def _merge_kernel(x_ref, y_ref, p_ref, wg0_ref, wg1_ref, wg2_ref, wb_ref, wo_ref, wpg_ref, wpe_ref, lng_ref,
                  lnb_ref, o_ref, m_s, r_s, *, alpha, rows_per_pass):
    passes = [slice(r0, r0 + rows_per_pass) for r0 in range(0, x_ref.shape[0], rows_per_pass)]

    def branch_merge(rows):
        xb = x_ref[rows, :].astype(BF16)
        merged = None
        for b, wg_ref in enumerate((wg0_ref, wg1_ref, wg2_ref)):
            gate = _sigmoid(jnp.dot(xb, wg_ref[...], preferred_element_type=F32))
            term = gate * jnp.dot(y_ref[rows, b * W_BRANCH:(b + 1) * W_BRANCH], wb_ref[b],
                                  preferred_element_type=F32)
            merged = term if merged is None else merged + term
        m_s[rows, :] = merged.astype(BF16)

    def residual(rows):
        r = alpha * x_ref[rows, :] + jnp.dot(m_s[rows, :], wo_ref[...], preferred_element_type=F32)
        pgate = _sigmoid(jnp.dot(r.astype(BF16), wpg_ref[...], preferred_element_type=F32))
        r_s[rows, :] = r + pgate * jnp.dot(p_ref[rows, :].astype(BF16), wpe_ref[...], preferred_element_type=F32)

    def layer_norm(rows):
        r = r_s[rows, :]
        mu = jnp.mean(r, axis=-1, keepdims=True)
        dev = r - mu
        var = jnp.mean(dev * dev, axis=-1, keepdims=True)
        o_ref[rows, :] = dev * lax.rsqrt(var + LN_EPS) * lng_ref[...] + lnb_ref[...]

    branch_merge(passes[0])
    residual(passes[0])
    for prev, rows in zip(passes[:-1], passes[1:]):
        branch_merge(rows)
        layer_norm(prev)
        residual(rows)
    layer_norm(passes[-1])


def _merge_call(layer, x2, y2, p3, w_in_b, merge_w, *, alpha):
    n, d = x2.shape
    tm = min(MERGE_TILE, n)
    rows_per_pass = min(MERGE_ROWS, tm)
    gate_block0 = N_MIX_COLS // d
    return pl.pallas_call(
        functools.partial(_merge_kernel, alpha=alpha, rows_per_pass=rows_per_pass),
        grid=(n // tm,),
        scratch_shapes=[pltpu.VMEM((tm, d), BF16),
                        pltpu.VMEM((tm, d), F32)],
        in_specs=[
            pl.BlockSpec((tm, d), lambda i: (i, 0)),
            pl.BlockSpec((tm, y2.shape[1]), lambda i: (i, 0)),
            pl.BlockSpec((None, tm, p3.shape[2]), lambda i: (layer, i, 0)),
        ] + [_layer_spec(w_in_b, layer, block=(d, d), index=(0, gate_block0 + b)) for b in range(3)]
          + [_layer_spec(a, layer) for a in merge_w],
        out_specs=pl.BlockSpec((tm, d), lambda i: (i, 0)),
        out_shape=jax.ShapeDtypeStruct((n, d), F32),
        compiler_params=pltpu.CompilerParams(
            dimension_semantics=("arbitrary",), vmem_limit_bytes=VMEM_LIMIT_BYTES),
        name="merge",
    )(x2, y2, p3, w_in_b, w_in_b, w_in_b, *merge_w)


def _tables(pos, blk):
    half = HEAD_DIM // 2
    inv = ROPE_BASE ** (-np.arange(half, dtype=np.float64) / half)
    ang = np.asarray(pos, np.float64)[:, None] * inv[None, :]
    c, s = np.cos(ang), np.sin(ang)
    cos = np.tile(np.concatenate([c, c], axis=-1), (1, N_HEADS))
    sin = np.tile(np.concatenate([-s, s], axis=-1), (1, N_HEADS))

    log_g = np.log1p(-np.exp2(-5.0 - np.arange(N_HEADS, dtype=np.float64)))
    i = np.arange(blk, dtype=np.float64)
    diff = i[:, None] - i[None, :]
    decay = np.where(diff >= 0, np.exp(log_g[:, None, None] * np.maximum(diff, 0.0)), 0.0)
    zeta = np.exp(log_g[:, None] * (blk - 1.0 - i)[None, :])
    xi = np.exp(log_g[:, None] * (i + 1.0)[None, :])
    g_blk = np.exp(log_g * blk)
    xi_full = np.repeat(xi.T, HEAD_DIM, axis=1)
    zeta_full = np.repeat(zeta.T, HEAD_DIM, axis=1)
    q = np.arange(QUAD)
    bmask = (q[:, None] // HEAD_DIM == q[None, :] // HEAD_DIM).astype(np.float64)
    gmat = np.stack([bmask * np.repeat(g_blk[4 * g:4 * g + 4], HEAD_DIM)[None, :] for g in range(N_QUADS)])
    lane = np.arange(W_BRANCH)
    seg = (lane[:, None] // HEAD_DIM == lane[None, :] // HEAD_DIM) / HEAD_DIM
    f32 = lambda a: jnp.asarray(a, F32)
    return (f32(cos), f32(sin), f32(decay), f32(xi_full), f32(zeta_full), f32(gmat), f32(bmask),
            jnp.asarray(seg, BF16))


def _block_diag(w):
    depth, n, bi, bj = w.shape
    tiled = jnp.tile(w.reshape(depth, n * bi, bj), (1, 1, n))
    r = np.arange(n * bi)[:, None] // bi
    c = np.arange(n * bj)[None, :] // bj
    return jnp.where(jnp.asarray(r == c)[None], tiled, 0.0)


def kernel(x_prompt, x_sample, p_prompt, p_sample, cache_k_a, cache_v_a, state_ret, state_conv, state_lru, w_in, rel_table, gn_gain, conv_w, conv_b, w_gate_a, b_gate_a, w_gate_x, b_gate_x, lru_lambda, w_branch, w_out, ln_gain, ln_bias, w_ple, w_ple_gate):
    depth = w_in.shape[0]
    bp, sp, d_model = x_prompt.shape
    bs, ss, _ = x_sample.shape
    assert cache_k_a.shape[2] == HIST and w_in.shape[2] == N_MIX_COLS + 3 * d_model
    assert N_MIX_COLS % d_model == 0
    tp = min(PROMPT_TILE, sp)
    assert sp % tp == 0 and tp % LANES == 0 and HIST % tp == 0 and ss % SUBLANES == 0
    alpha = float((2 * depth) ** 0.25)

    w_in_b = w_in.astype(BF16)
    row = lambda a: a.reshape(depth, 1, a.shape[-1])
    layer_w = (row(gn_gain), conv_w, row(conv_b), _block_diag(w_gate_a).astype(BF16), row(b_gate_a),
               _block_diag(w_gate_x).astype(BF16), row(b_gate_x), row(lru_lambda))
    merge_w = (w_branch.astype(BF16), w_out.astype(BF16), w_ple_gate.astype(BF16), w_ple.astype(BF16),
               row(ln_gain), row(ln_bias))

    bias_p, bias_s = _expand_bias(rel_table, tp, ss)
    tables_p = _tables(np.arange(sp), tp)
    tables_s = _tables(PAST_LEN + np.arange(ss), ss)
    cache = (cache_k_a.reshape(depth, bs, HIST, W_BRANCH), cache_v_a.reshape(depth, bs, HIST, W_BRANCH),
             state_ret, state_conv, state_lru.reshape(depth, bs, 1, W_BRANCH))
    pp = p_prompt.reshape(depth, bp * sp, -1)
    ps = p_sample.reshape(depth, bs * ss, -1)

    hp, hs = x_prompt, x_sample
    outs_p = outs_s = None
    for l in range(depth):
        y, *outs_p = _mixer_call(l, hp, tables_p, w_in_b, bias_p, layer_w, None, outs_p, T=tp, prompt=True)
        hp = _merge_call(l, hp.reshape(bp * sp, d_model), y.reshape(bp * sp, -1), pp, w_in_b, merge_w,
                         alpha=alpha).reshape(bp, sp, d_model)
        y, *outs_s = _mixer_call(l, hs, tables_s, w_in_b, bias_s, layer_w, cache, outs_s, T=ss, prompt=False)
        hs = _merge_call(l, hs.reshape(bs * ss, d_model), y.reshape(bs * ss, -1), ps, w_in_b, merge_w,
                         alpha=alpha).reshape(bs, ss, d_model)

    k_p, v_p, r_p, c_p, s_p = outs_p
    k_s, v_s, r_s, c_s, s_s = outs_s
    heads = lambda a: a.reshape(a.shape[:-1] + (N_HEADS, HEAD_DIM))
    return (hp, hs, heads(k_p), heads(v_p), heads(k_s), heads(v_s), r_p, r_s, c_p, c_s,
            s_p.reshape(depth, bp, W_BRANCH), s_s.reshape(depth, bs, W_BRANCH))
```

```python
import functools

import numpy as np

import jax
import jax.numpy as jnp
from jax import lax
from jax.experimental import pallas as pl
from jax.experimental.pallas import tpu as pltpu

F32 = jnp.float32
BF16 = jnp.bfloat16

CHUNK = 64
N_LEFT_CHUNKS = 8
HIST = N_LEFT_CHUNKS * CHUNK
HEAD_DIM = 64
N_HEADS = 8
W_BRANCH = N_HEADS * HEAD_DIM
N_PAIRS = N_HEADS // 2
LANES = 128
SUBLANES = 8
QUAD = 4 * HEAD_DIM
N_QUADS = N_HEADS // 4
REL_CLIP = 128
CONV_W = 4
LRU_C = 8.0
ROPE_BASE = 10000.0
LN_EPS = 1e-5
NEG_INF = -1e30
LOG2E = 1.4426950408889634
PAST_LEN = 1024
N_MIX_COLS = 10 * W_BRANCH
PROMPT_TILE = 256
MERGE_TILE = 1024
MERGE_ROWS = 256
SOFTMAX_ROWS = 16
SQRT_FLOOR = 1e-30
VMEM_LIMIT_BYTES = 56 * 1024 * 1024

_NT = (((1,), (1,)), ((), ()))
_TN = (((0,), (0,)), ((), ()))


def _const_spec(shape):
    zeros = (0,) * len(shape)
    return pl.BlockSpec(shape, lambda *_: zeros, pipeline_mode=pl.Buffered(1))


def _layer_spec(arr, layer, block=None, index=None):
    block = tuple(arr.shape[1:]) if block is None else tuple(block)
    index = (0,) * len(block) if index is None else tuple(index)
    return pl.BlockSpec((None,) + block, lambda *_: (layer,) + index, pipeline_mode=pl.Buffered(1))


def _sigmoid(x):
    return 0.5 * jnp.tanh(0.5 * x) + 0.5


def _silu(x):
    half = 0.5 * x
    return half * jnp.tanh(half) + half


def _bias_kernel(tab_ref, bp_ref, bs_ref, *, tq, ts):
    h = pl.program_id(0)

    def toeplitz_block(rows, i0, j0):
        i = lax.broadcasted_iota(jnp.int32, (rows, LANES), 0) + i0
        j = lax.broadcasted_iota(jnp.int32, (rows, LANES), 1) + j0
        idx = jnp.clip(HIST + i - j, -REL_CLIP, REL_CLIP) + REL_CLIP
        d_lo = HIST + i0 - (j0 + LANES - 1)
        d_hi = HIST + i0 + rows - 1 - j0
        k_lo = min(max(d_lo, -REL_CLIP), REL_CLIP) + REL_CLIP
        k_hi = min(max(d_hi, -REL_CLIP), REL_CLIP) + REL_CLIP

        def body(k, acc):
            return jnp.where(idx == k, tab_ref[h, k], acc)

        blk = lax.fori_loop(k_lo, k_hi + 1, body, jnp.zeros((rows, LANES), F32), unroll=8)
        return blk * LOG2E, i, j

    for ib in range(tq // LANES):
        for jb in range((HIST + tq) // LANES):
            blk, i, j = toeplitz_block(LANES, ib * LANES, jb * LANES)
            qc = i // CHUNK
            kc = j // CHUNK
            vis = (kc >= qc) & (kc <= qc + N_LEFT_CHUNKS)
            bp_ref[0, ib * LANES:(ib + 1) * LANES, jb * LANES:(jb + 1) * LANES] = jnp.where(vis, blk, NEG_INF)
    for jb in range(bs_ref.shape[2] // LANES):
        blk, _, _ = toeplitz_block(ts, 0, jb * LANES)
        bs_ref[0, :, jb * LANES:(jb + 1) * LANES] = blk


def _expand_bias(rel_table, tq, ts):
    depth, nh, nrel = rel_table.shape
    ws = -(-(HIST + ts) // LANES) * LANES
    bp, bs = pl.pallas_call(
        functools.partial(_bias_kernel, tq=tq, ts=ts),
        grid=(depth * nh,),
        in_specs=[pl.BlockSpec(memory_space=pltpu.SMEM)],
        out_specs=[pl.BlockSpec((1, tq, HIST + tq), lambda g: (g, 0, 0)),
                   pl.BlockSpec((1, ts, ws), lambda g: (g, 0, 0))],
        out_shape=[jax.ShapeDtypeStruct((depth * nh, tq, HIST + tq), F32),
                   jax.ShapeDtypeStruct((depth * nh, ts, ws), F32)],
        name="rel_bias",
    )(rel_table.reshape(depth * nh, nrel))
    return bp.reshape(depth, nh, tq, HIST + tq), bs.reshape(depth, nh, ts, ws)


def _mixer_kernel(*refs, T, prompt, n_tiles, n_alias):
    n_in = 19 if prompt else 24
    (x_ref, cos_ref, sin_ref, w_ref, bias_ref, dec_ref, xi_ref, zeta_ref, gmat_ref, bmask_ref, seg_ref, gn_ref,
     cw_ref, cb_ref, wga_ref, bga_ref, wgx_ref, bgx_ref, lam_ref) = refs[:19]
    if not prompt:
        kc_ref, vc_ref, s0_ref, c0_ref, h0_ref = refs[19:24]
    refs = refs[n_in + n_alias:]
    (y_ref, ko_ref, vo_ref, ret_ref, conv_ref, lru_ref,
     xb_s, qm_s, kbuf_s, vbuf_s, ya_s, yb_s, st_s, xbuf, hseq_s, h_s, sc_s, pr_s, rl_s) = refs

    t = pl.program_id(1)
    tail = CONV_W - 1
    lane = lax.broadcasted_iota(jnp.int32, (T, LANES), 1)
    first_head = lane < HEAD_DIM

    def split_heads(pair):
        zero = jnp.zeros_like(pair)
        return jnp.where(first_head, pair, zero), jnp.where(first_head, zero, pair)

    hist_tiles = HIST // T
    n_slots = hist_tiles + 1
    own_slot = lax.rem(t, n_slots) if prompt else 0
    own_rows = slice(0, T) if prompt else slice(HIST, HIST + T)

    if prompt:
        @pl.when(t == 0)
        def _():
            st_s[...] = jnp.zeros_like(st_s)
            xbuf[0:SUBLANES, :] = jnp.zeros((SUBLANES, W_BRANCH), F32)
            h_s[...] = jnp.zeros_like(h_s)
    else:
        kc = kc_ref[0].astype(BF16)
        vc = vc_ref[0].astype(BF16)
        for hp in range(N_PAIRS):
            sl = slice(hp * LANES, (hp + 1) * LANES)
            kbuf_s[0, hp, 0:HIST, :] = kc[:, sl]
            vbuf_s[0, hp, 0:HIST, :] = vc[:, sl]
        st_s[...] = jnp.zeros_like(st_s)
        for h in range(N_HEADS):
            d0 = (h % 4) * HEAD_DIM
            st_s[h // 4, d0:d0 + HEAD_DIM, d0:d0 + HEAD_DIM] = s0_ref[0, h]
        xbuf[0:SUBLANES, :] = jnp.zeros((SUBLANES, W_BRANCH), F32)
        xbuf[SUBLANES - tail:SUBLANES, :] = c0_ref[0]
        h_s[...] = h0_ref[0]

    xb_s[...] = x_ref[0].astype(BF16)

    def proj(c):
        cols = slice(c * W_BRANCH, (c + 1) * W_BRANCH)
        return jnp.dot(xb_s[...], w_ref[:, cols], preferred_element_type=F32)

    def own_keys(c, out_ref, buf_s):
        full = proj(c)
        out_ref[0] = full
        half = full.astype(BF16)
        for hp in range(N_PAIRS):
            buf_s[own_slot, hp, own_rows, :] = half[:, hp * LANES:(hp + 1) * LANES]

    xr = proj(8)
    xbuf[SUBLANES:SUBLANES + T, :] = xr
    xc = cb_ref[...] + xr * cw_ref[tail:CONV_W, :]
    for j in range(tail):
        xc = xc + xbuf[SUBLANES - tail + j:SUBLANES - tail + j + T, :] * cw_ref[j:j + 1, :]
    xcb = xc.astype(BF16)
    gate_a = jnp.dot(xcb, wga_ref[...], preferred_element_type=F32)
    gate_x = jnp.dot(xcb, wgx_ref[...], preferred_element_type=F32)
    qa = (proj(0) * (HEAD_DIM ** -0.5 * LOG2E)).astype(BF16)
    for hp in range(N_PAIRS):
        qm_s[2 * hp], qm_s[2 * hp + 1] = split_heads(qa[:, hp * LANES:(hp + 1) * LANES])
    r = _sigmoid(gate_a + bga_ref[...])
    ig = _sigmoid(gate_x + bgx_ref[...])
    log_a = (-LRU_C) * r * jax.nn.softplus(-lam_ref[...])
    a_t = jnp.exp(log_a)
    gap = 1.0 - a_t * a_t
    b_t = gap * lax.rsqrt(jnp.maximum(gap, SQRT_FLOOR)) * ig * xc
    n_groups = T // SUBLANES
    a3 = a_t.reshape(n_groups, SUBLANES, W_BRANCH)
    b3 = b_t.reshape(n_groups, SUBLANES, W_BRANCH)
    sub = lax.broadcasted_iota(jnp.int32, (n_groups, SUBLANES, W_BRANCH), 1)
    between_steps = {1: functools.partial(own_keys, 1, ko_ref, kbuf_s),
                     2: functools.partial(own_keys, 2, vo_ref, vbuf_s)}
    step = 1
    while step < SUBLANES:
        if step in between_steps:
            between_steps[step]()
        keep = sub >= step
        a_prev = jnp.where(keep, pltpu.roll(a3, step, 1), 1.0)
        b_prev = jnp.where(keep, pltpu.roll(b3, step, 1), 0.0)
        b3 = a3 * b_prev + b3
        a3 = a3 * a_prev
        step *= 2
    gated_c = _silu(proj(9))
    carry = h_s[...]
    for g in range(n_groups):
        hg = a3[g] * carry + b3[g]
        hseq_s[g * SUBLANES:(g + 1) * SUBLANES, :] = hg
        carry = hg[SUBLANES - 1:SUBLANES, :]
    h_s[...] = carry
    y_ref[0, :, 2 * W_BRANCH:3 * W_BRANCH] = (hseq_s[...] * gated_c).astype(BF16)
    xbuf[0:SUBLANES, :] = xbuf[T:T + SUBLANES, :]


    def attend(hist_tiles_present):
        if prompt:
            blocks = [(lax.rem(t + n_slots - back, n_slots), slice(0, T), slice(HIST - back * T, HIST - (back - 1) * T))
                      for back in range(hist_tiles_present, 0, -1)]
        else:
            blocks = [(0, slice(0, HIST), slice(0, HIST))]
        blocks.append((own_slot, own_rows, slice(HIST, HIST + T)))
        c0 = blocks[0][2].start

        def scores(h):
            for slot, rows, cols in blocks:
                sc_s[h % 2, :, cols] = (lax.dot_general(qm_s[h], kbuf_s[slot, h // 2, rows, :], _NT,
                                                        preferred_element_type=F32) + bias_ref[h, :, cols])

        def softmax(h):
            for r0 in range(0, T, SOFTMAX_ROWS):
                rows = slice(r0, r0 + SOFTMAX_ROWS)
                s = sc_s[h % 2, rows, c0:HIST + T]
                p = jnp.exp2(s - jnp.max(s, axis=1, keepdims=True))
                pr_s[h % 2, rows, c0:HIST + T] = p.astype(BF16)
                rl_s[h % 2, rows, :] = jnp.broadcast_to(1.0 / jnp.sum(p, axis=1, keepdims=True),
                                                        (SOFTMAX_ROWS, LANES))

        def weighted_values(h):
            sl = slice((h // 2) * LANES, (h // 2 + 1) * LANES)
            o = None
            for slot, rows, cols in blocks:
                part = jnp.dot(pr_s[h % 2, :, cols], vbuf_s[slot, h // 2, rows, :], preferred_element_type=F32)
                o = part if o is None else o + part
            o = o * rl_s[h % 2]
            ya_s[:, sl] = o if h % 2 == 0 else jnp.where(first_head, ya_s[:, sl], o)

        scores(0)
        for h in range(N_HEADS):
            if h + 1 < N_HEADS:
                scores(h + 1)
            softmax(h)
            weighted_values(h)

    if prompt:
        for k in range(hist_tiles):
            pl.when(t == k)(functools.partial(attend, k))
        pl.when(t >= hist_tiles)(functools.partial(attend, hist_tiles))
    else:
        attend(None)

    y_ref[0, :, 0:W_BRANCH] = (ya_s[...] * _silu(proj(3))).astype(BF16)

    cos = cos_ref[...]
    sin = sin_ref[...]
    lane_w = lax.broadcasted_iota(jnp.int32, (T, W_BRANCH), 1)
    low_half = (lane_w % HEAD_DIM) < (HEAD_DIM // 2)

    def rope(v):
        partner = jnp.where(low_half, pltpu.roll(v, W_BRANCH - HEAD_DIM // 2, 1), pltpu.roll(v, HEAD_DIM // 2, 1))
        return v * cos + partner * sin

    qr = rope(proj(4))
    kr = rope(proj(5)) * (HEAD_DIM ** -0.5)
    q_in = qr.astype(BF16)
    q_x = (qr * xi_ref[...]).astype(BF16)
    k_in = kr.astype(BF16)
    k_z = (kr * zeta_ref[...]).astype(BF16)
    v_b = proj(6).astype(BF16)
    gated_b = _silu(proj(7))
    for g in range(N_QUADS):
        gs = slice(g * QUAD, (g + 1) * QUAD)
        state = st_s[g]
        yb_s[:, gs] = jnp.dot(q_x[:, gs], state.astype(BF16), preferred_element_type=F32)
        outer = lax.dot_general(k_z[:, gs], v_b[:, gs], _TN, preferred_element_type=F32)
        st_s[g] = gmat_ref[g] * state + bmask_ref[...] * outer
    for hp in range(N_PAIRS):
        sl = slice(hp * LANES, (hp + 1) * LANES)
        k_heads = split_heads(k_in[:, sl])
        v_heads = split_heads(v_b[:, sl])
        acc = yb_s[:, sl]
        for e in range(2):
            h = 2 * hp + e
            qk = lax.dot_general(q_in[:, sl], k_heads[e], _NT, preferred_element_type=F32)
            inner = (qk * dec_ref[h]).astype(BF16)
            acc = acc + jnp.dot(inner, v_heads[e], preferred_element_type=F32)
        yb_s[:, sl] = acc

    def seg_mean(a):
        return jnp.dot(a.astype(BF16), seg_ref[...], preferred_element_type=F32)

    yb = yb_s[...]
    dev = yb - seg_mean(yb)
    var = seg_mean(dev * dev)
    ybn = dev * lax.rsqrt(var + LN_EPS) * gn_ref[...]
    y_ref[0, :, W_BRANCH:2 * W_BRANCH] = (ybn * gated_b).astype(BF16)

    def write_state():
        for h in range(N_HEADS):
            d0 = (h % 4) * HEAD_DIM
            ret_ref[0, h] = st_s[h // 4, d0:d0 + HEAD_DIM, d0:d0 + HEAD_DIM]
        conv_ref[0] = xbuf[SUBLANES - tail:SUBLANES, :]
        lru_ref[0] = h_s[...]

    if prompt:
        pl.when(t == n_tiles - 1)(write_state)
    else:
        write_state()


def _mixer_call(layer, x, tables, w_in_b, bias, layer_w, cache, prev, *, T, prompt):
    B, S, D = x.shape
    depth = w_in_b.shape[0]
    n_tiles = S // T
    kv_rows = min(HIST, S)
    kv_first = n_tiles - kv_rows // T
    cos, sin, dec, xi, zeta, gmat, bmask, seg = tables
    gn, cw, cb, wga, bga, wgx, bgx, lam = layer_w

    in_specs = [
        pl.BlockSpec((1, T, D), lambda b, t: (b, t, 0)),
        pl.BlockSpec((T, W_BRANCH), lambda b, t: (t, 0)),
        pl.BlockSpec((T, W_BRANCH), lambda b, t: (t, 0)),
        _layer_spec(w_in_b, layer, block=(D, N_MIX_COLS)),
        _layer_spec(bias, layer),
        _const_spec(dec.shape), _const_spec(xi.shape), _const_spec(zeta.shape),
        _const_spec(gmat.shape), _const_spec(bmask.shape), _const_spec(seg.shape),
    ] + [_layer_spec(a, layer) for a in layer_w]
    args = [x, cos, sin, w_in_b, bias, dec, xi, zeta, gmat, bmask, seg, gn, cw, cb, wga, bga, wgx, bgx, lam]
    if not prompt:
        for a in cache:
            blk = (1,) + tuple(a.shape[2:])
            zeros = (0,) * (len(blk) - 1)
            in_specs.append(pl.BlockSpec((None,) + blk, lambda b, t, zeros=zeros: (layer, b) + zeros))
        args += list(cache)
    aliases = {}
    if prev is not None:
        for k, a in enumerate(prev):
            in_specs.append(pl.BlockSpec(memory_space=pl.ANY))
            aliases[len(args)] = k + 1
            args.append(a)

    def kv_map(b, t):
        return (layer, b, jnp.maximum(t - kv_first, 0), 0)

    out_specs = [
        pl.BlockSpec((1, T, 3 * W_BRANCH), lambda b, t: (b, t, 0)),
        pl.BlockSpec((None, 1, T, W_BRANCH), kv_map),
        pl.BlockSpec((None, 1, T, W_BRANCH), kv_map),
        pl.BlockSpec((None, 1, N_HEADS, HEAD_DIM, HEAD_DIM), lambda b, t: (layer, b, 0, 0, 0)),
        pl.BlockSpec((None, 1, CONV_W - 1, W_BRANCH), lambda b, t: (layer, b, 0, 0)),
        pl.BlockSpec((None, 1, 1, W_BRANCH), lambda b, t: (layer, b, 0, 0)),
    ]
    out_shape = [
        jax.ShapeDtypeStruct((B, S, 3 * W_BRANCH), BF16),
        jax.ShapeDtypeStruct((depth, B, kv_rows, W_BRANCH), F32),
        jax.ShapeDtypeStruct((depth, B, kv_rows, W_BRANCH), F32),
        jax.ShapeDtypeStruct((depth, B, N_HEADS, HEAD_DIM, HEAD_DIM), F32),
        jax.ShapeDtypeStruct((depth, B, CONV_W - 1, W_BRANCH), F32),
        jax.ShapeDtypeStruct((depth, B, 1, W_BRANCH), F32),
    ]
    kv_buf_shape = (HIST // T + 1, N_PAIRS, T, LANES) if prompt else (1, N_PAIRS, HIST + T, LANES)
    scratch = [
        pltpu.VMEM((T, D), BF16),
        pltpu.VMEM((N_HEADS, T, LANES), BF16),
        pltpu.VMEM(kv_buf_shape, BF16),
        pltpu.VMEM(kv_buf_shape, BF16),
        pltpu.VMEM((T, W_BRANCH), F32),
        pltpu.VMEM((T, W_BRANCH), F32),
        pltpu.VMEM((N_QUADS, QUAD, QUAD), F32),
        pltpu.VMEM((T + SUBLANES, W_BRANCH), F32),
        pltpu.VMEM((T, W_BRANCH), F32),
        pltpu.VMEM((1, W_BRANCH), F32),
        pltpu.VMEM((2, T, HIST + T), F32),
        pltpu.VMEM((2, T, HIST + T), BF16),
        pltpu.VMEM((2, T, LANES), F32),
    ]
    return pl.pallas_call(
        functools.partial(_mixer_kernel, T=T, prompt=prompt, n_tiles=n_tiles, n_alias=len(aliases)),
        grid=(B, n_tiles),
        in_specs=in_specs,
        out_specs=out_specs,
        out_shape=out_shape,
        scratch_shapes=scratch,
        input_output_aliases=aliases,
        compiler_params=pltpu.CompilerParams(
            dimension_semantics=("arbitrary", "arbitrary"), vmem_limit_bytes=VMEM_LIMIT_BYTES,
            ),
        name="mixer_prompt" if prompt else "mixer_sample",
    )(*args)


def _merge_kernel(x_ref, y_ref, p_ref, wg0_ref, wg1_ref, wg2_ref, wb_ref, wo_ref, wpg_ref, wpe_ref, lng_ref,
                  lnb_ref, o_ref, m_s, r_s, *, alpha, rows_per_pass):
    passes = [slice(r0, r0 + rows_per_pass) for r0 in range(0, x_ref.shape[0], rows_per_pass)]

    def branch_merge(rows):
        xb = x_ref[rows, :].astype(BF16)
        merged = None
        for b, wg_ref in enumerate((wg0_ref, wg1_ref, wg2_ref)):
            gate = _sigmoid(jnp.dot(xb, wg_ref[...], preferred_element_type=F32))
            term = gate * jnp.dot(y_ref[rows, b * W_BRANCH:(b + 1) * W_BRANCH], wb_ref[b],
                                  preferred_element_type=F32)
            merged = term if merged is None else merged + term
        m_s[rows, :] = merged.astype(BF16)

    def residual(rows):
        r = alpha * x_ref[rows, :] + jnp.dot(m_s[rows, :], wo_ref[...], preferred_element_type=F32)
        pgate = _sigmoid(jnp.dot(r.astype(BF16), wpg_ref[...], preferred_element_type=F32))
        r_s[rows, :] = r + pgate * jnp.dot(p_ref[rows, :].astype(BF16), wpe_ref[...], preferred_element_type=F32)

    def layer_norm(rows):
        r = r_s[rows, :]
        mu = jnp.mean(r, axis=-1, keepdims=True)
        dev = r - mu
        var = jnp.mean(dev * dev, axis=-1, keepdims=True)
        o_ref[rows, :] = dev * lax.rsqrt(var + LN_EPS) * lng_ref[...] + lnb_ref[...]

    branch_merge(passes[0])
    residual(passes[0])
    for prev, rows in zip(passes[:-1], passes[1:]):
        branch_merge(rows)
        layer_norm(prev)
        residual(rows)
    layer_norm(passes[-1])


def _merge_call(layer, x2, y2, p3, w_in_b, merge_w, *, alpha):
    n, d = x2.shape
    tm = min(MERGE_TILE, n)
    rows_per_pass = min(MERGE_ROWS, tm)
    gate_block0 = N_MIX_COLS // d
    return pl.pallas_call(
        functools.partial(_merge_kernel, alpha=alpha, rows_per_pass=rows_per_pass),
        grid=(n // tm,),
        scratch_shapes=[pltpu.VMEM((tm, d), BF16),
                        pltpu.VMEM((tm, d), F32)],
        in_specs=[
            pl.BlockSpec((tm, d), lambda i: (i, 0)),
            pl.BlockSpec((tm, y2.shape[1]), lambda i: (i, 0)),
            pl.BlockSpec((None, tm, p3.shape[2]), lambda i: (layer, i, 0)),
        ] + [_layer_spec(w_in_b, layer, block=(d, d), index=(0, gate_block0 + b)) for b in range(3)]
          + [_layer_spec(a, layer) for a in merge_w],
        out_specs=pl.BlockSpec((tm, d), lambda i: (i, 0)),
        out_shape=jax.ShapeDtypeStruct((n, d), F32),
        compiler_params=pltpu.CompilerParams(
            dimension_semantics=("arbitrary",), vmem_limit_bytes=VMEM_LIMIT_BYTES),
        name="merge",
    )(x2, y2, p3, w_in_b, w_in_b, w_in_b, *merge_w)


def _tables(pos, blk):
    half = HEAD_DIM // 2
    inv = ROPE_BASE ** (-np.arange(half, dtype=np.float64) / half)
    ang = np.asarray(pos, np.float64)[:, None] * inv[None, :]
    c, s = np.cos(ang), np.sin(ang)
    cos = np.tile(np.concatenate([c, c], axis=-1), (1, N_HEADS))
    sin = np.tile(np.concatenate([-s, s], axis=-1), (1, N_HEADS))

    log_g = np.log1p(-np.exp2(-5.0 - np.arange(N_HEADS, dtype=np.float64)))
    i = np.arange(blk, dtype=np.float64)
    diff = i[:, None] - i[None, :]
    decay = np.where(diff >= 0, np.exp(log_g[:, None, None] * np.maximum(diff, 0.0)), 0.0)
    zeta = np.exp(log_g[:, None] * (blk - 1.0 - i)[None, :])
    xi = np.exp(log_g[:, None] * (i + 1.0)[None, :])
    g_blk = np.exp(log_g * blk)
    xi_full = np.repeat(xi.T, HEAD_DIM, axis=1)
    zeta_full = np.repeat(zeta.T, HEAD_DIM, axis=1)
    q = np.arange(QUAD)
    bmask = (q[:, None] // HEAD_DIM == q[None, :] // HEAD_DIM).astype(np.float64)
    gmat = np.stack([bmask * np.repeat(g_blk[4 * g:4 * g + 4], HEAD_DIM)[None, :] for g in range(N_QUADS)])
    lane = np.arange(W_BRANCH)
    seg = (lane[:, None] // HEAD_DIM == lane[None, :] // HEAD_DIM) / HEAD_DIM
    f32 = lambda a: jnp.asarray(a, F32)
    return (f32(cos), f32(sin), f32(decay), f32(xi_full), f32(zeta_full), f32(gmat), f32(bmask),
            jnp.asarray(seg, BF16))


def _block_diag(w):
    depth, n, bi, bj = w.shape
    tiled = jnp.tile(w.reshape(depth, n * bi, bj), (1, 1, n))
    r = np.arange(n * bi)[:, None] // bi
    c = np.arange(n * bj)[None, :] // bj
    return jnp.where(jnp.asarray(r == c)[None], tiled, 0.0)


def kernel(x_prompt, x_sample, p_prompt, p_sample, cache_k_a, cache_v_a, state_ret, state_conv, state_lru, w_in, rel_table, gn_gain, conv_w, conv_b, w_gate_a, b_gate_a, w_gate_x, b_gate_x, lru_lambda, w_branch, w_out, ln_gain, ln_bias, w_ple, w_ple_gate):
    depth = w_in.shape[0]
    bp, sp, d_model = x_prompt.shape
    bs, ss, _ = x_sample.shape
    assert cache_k_a.shape[2] == HIST and w_in.shape[2] == N_MIX_COLS + 3 * d_model
    assert N_MIX_COLS % d_model == 0
    tp = min(PROMPT_TILE, sp)
    assert sp % tp == 0 and tp % LANES == 0 and HIST % tp == 0 and ss % SUBLANES == 0
    alpha = float((2 * depth) ** 0.25)

    w_in_b = w_in.astype(BF16)
    row = lambda a: a.reshape(depth, 1, a.shape[-1])
    layer_w = (row(gn_gain), conv_w, row(conv_b), _block_diag(w_gate_a).astype(BF16), row(b_gate_a),
               _block_diag(w_gate_x).astype(BF16), row(b_gate_x), row(lru_lambda))
    merge_w = (w_branch.astype(BF16), w_out.astype(BF16), w_ple_gate.astype(BF16), w_ple.astype(BF16),
               row(ln_gain), row(ln_bias))

    bias_p, bias_s = _expand_bias(rel_table, tp, ss)
    tables_p = _tables(np.arange(sp), tp)
    tables_s = _tables(PAST_LEN + np.arange(ss), ss)
    cache = (cache_k_a.reshape(depth, bs, HIST, W_BRANCH), cache_v_a.reshape(depth, bs, HIST, W_BRANCH),
             state_ret, state_conv, state_lru.reshape(depth, bs, 1, W_BRANCH))
    pp = p_prompt.reshape(depth, bp * sp, -1)
    ps = p_sample.reshape(depth, bs * ss, -1)

    hp, hs = x_prompt, x_sample
    outs_p = outs_s = None
    for l in range(depth):
        y, *outs_p = _mixer_call(l, hp, tables_p, w_in_b, bias_p, layer_w, None, outs_p, T=tp, prompt=True)
        hp = _merge_call(l, hp.reshape(bp * sp, d_model), y.reshape(bp * sp, -1), pp, w_in_b, merge_w,
                         alpha=alpha).reshape(bp, sp, d_model)
        y, *outs_s = _mixer_call(l, hs, tables_s, w_in_b, bias_s, layer_w, cache, outs_s, T=ss, prompt=False)
        hs = _merge_call(l, hs.reshape(bs * ss, d_model), y.reshape(bs * ss, -1), ps, w_in_b, merge_w,
                         alpha=alpha).reshape(bs, ss, d_model)

    k_p, v_p, r_p, c_p, s_p = outs_p
    k_s, v_s, r_s, c_s, s_s = outs_s
    heads = lambda a: a.reshape(a.shape[:-1] + (N_HEADS, HEAD_DIM))
    return (hp, hs, heads(k_p), heads(v_p), heads(k_s), heads(v_s), r_p, r_s, c_p, c_s,
            s_p.reshape(depth, bp, W_BRANCH), s_s.reshape(depth, bs, W_BRANCH))
```

```python
import functools

import numpy as np

import jax
import jax.numpy as jnp
from jax import lax
from jax.experimental import pallas as pl
from jax.experimental.pallas import tpu as pltpu

F32 = jnp.float32
BF16 = jnp.bfloat16

CHUNK = 64
N_LEFT_CHUNKS = 8
HIST = N_LEFT_CHUNKS * CHUNK
HEAD_DIM = 64
N_HEADS = 8
W_BRANCH = N_HEADS * HEAD_DIM
N_PAIRS = N_HEADS // 2
LANES = 128
SUBLANES = 8
QUAD = 4 * HEAD_DIM
N_QUADS = N_HEADS // 4
REL_CLIP = 128
CONV_W = 4
LRU_C = 8.0
ROPE_BASE = 10000.0
LN_EPS = 1e-5
NEG_INF = -1e30
LOG2E = 1.4426950408889634
PAST_LEN = 1024
N_MIX_COLS = 10 * W_BRANCH
PROMPT_TILE = 256
MERGE_TILE = 1024
MERGE_ROWS = 256
SOFTMAX_ROWS = 16
SQRT_FLOOR = 1e-30
VMEM_LIMIT_BYTES = 56 * 1024 * 1024

_NT = (((1,), (1,)), ((), ()))
_TN = (((0,), (0,)), ((), ()))


def _const_spec(shape):
    zeros = (0,) * len(shape)
    return pl.BlockSpec(shape, lambda *_: zeros, pipeline_mode=pl.Buffered(1))


def _layer_spec(arr, layer, block=None, index=None):
    block = tuple(arr.shape[1:]) if block is None else tuple(block)
    index = (0,) * len(block) if index is None else tuple(index)
    return pl.BlockSpec((None,) + block, lambda *_: (layer,) + index, pipeline_mode=pl.Buffered(1))


def _sigmoid(x):
    return 0.5 * jnp.tanh(0.5 * x) + 0.5


def _silu(x):
    half = 0.5 * x
    return half * jnp.tanh(half) + half


def _bias_kernel(tab_ref, bp_ref, bs_ref, *, tq, ts):
    h = pl.program_id(0)

    blocks = {}

    def toeplitz_block(rows, i0, j0):
        i = lax.broadcasted_iota(jnp.int32, (rows, LANES), 0) + i0
        j = lax.broadcasted_iota(jnp.int32, (rows, LANES), 1) + j0
        if (rows, i0 - j0) in blocks:
            return blocks[rows, i0 - j0], i, j
        idx = jnp.clip(HIST + i - j, -REL_CLIP, REL_CLIP) + REL_CLIP
        d_lo = HIST + i0 - (j0 + LANES - 1)
        d_hi = HIST + i0 + rows - 1 - j0
        k_lo = min(max(d_lo, -REL_CLIP), REL_CLIP) + REL_CLIP
        k_hi = min(max(d_hi, -REL_CLIP), REL_CLIP) + REL_CLIP

        def body(k, acc):
            return jnp.where(idx == k, tab_ref[h, k], acc)

        blk = lax.fori_loop(k_lo, k_hi + 1, body, jnp.zeros((rows, LANES), F32), unroll=8) * LOG2E
        blocks[rows, i0 - j0] = blk
        return blk, i, j

    for ib in range(tq // LANES):
        for jb in range((HIST + tq) // LANES):
            blk, i, j = toeplitz_block(LANES, ib * LANES, jb * LANES)
            qc = i // CHUNK
            kc = j // CHUNK
            vis = (kc >= qc) & (kc <= qc + N_LEFT_CHUNKS)
            bp_ref[0, ib * LANES:(ib + 1) * LANES, jb * LANES:(jb + 1) * LANES] = jnp.where(vis, blk, NEG_INF)
    for jb in range(bs_ref.shape[2] // LANES):
        blk, _, _ = toeplitz_block(ts, 0, jb * LANES)
        bs_ref[0, :, jb * LANES:(jb + 1) * LANES] = blk


def _expand_bias(rel_table, tq, ts):
    depth, nh, nrel = rel_table.shape
    ws = -(-(HIST + ts) // LANES) * LANES
    bp, bs = pl.pallas_call(
        functools.partial(_bias_kernel, tq=tq, ts=ts),
        grid=(depth * nh,),
        in_specs=[pl.BlockSpec(memory_space=pltpu.SMEM)],
        out_specs=[pl.BlockSpec((1, tq, HIST + tq), lambda g: (g, 0, 0)),
                   pl.BlockSpec((1, ts, ws), lambda g: (g, 0, 0))],
        out_shape=[jax.ShapeDtypeStruct((depth * nh, tq, HIST + tq), F32),
                   jax.ShapeDtypeStruct((depth * nh, ts, ws), F32)],
        name="rel_bias",
    )(rel_table.reshape(depth * nh, nrel))
    return bp.reshape(depth, nh, tq, HIST + tq), bs.reshape(depth, nh, ts, ws)


def _mixer_kernel(*refs, T, prompt, n_tiles, n_alias):
    n_in = 19 if prompt else 24
    (x_ref, cos_ref, sin_ref, w_ref, bias_ref, dec_ref, xi_ref, zeta_ref, gmat_ref, bmask_ref, seg_ref, gn_ref,
     cw_ref, cb_ref, wga_ref, bga_ref, wgx_ref, bgx_ref, lam_ref) = refs[:19]
    if not prompt:
        kc_ref, vc_ref, s0_ref, c0_ref, h0_ref = refs[19:24]
    refs = refs[n_in + n_alias:]
    (y_ref, ko_ref, vo_ref, ret_ref, conv_ref, lru_ref,
     xb_s, qm_s, kbuf_s, vbuf_s, ya_s, yb_s, st_s, xbuf, hseq_s, h_s, sc_s, pr_s, rl_s) = refs

    t = pl.program_id(1)
    tail = CONV_W - 1
    lane = lax.broadcasted_iota(jnp.int32, (T, LANES), 1)
    first_head = lane < HEAD_DIM

    def split_heads(pair):
        zero = jnp.zeros_like(pair)
        return jnp.where(first_head, pair, zero), jnp.where(first_head, zero, pair)

    hist_tiles = HIST // T
    n_slots = hist_tiles + 1
    own_slot = lax.rem(t, n_slots) if prompt else 0
    own_rows = slice(0, T) if prompt else slice(HIST, HIST + T)

    if prompt:
        @pl.when(t == 0)
        def _():
            st_s[...] = jnp.zeros_like(st_s)
            xbuf[0:SUBLANES, :] = jnp.zeros((SUBLANES, W_BRANCH), F32)
            h_s[...] = jnp.zeros_like(h_s)
    else:
        kc = kc_ref[0].astype(BF16)
        vc = vc_ref[0].astype(BF16)
        for hp in range(N_PAIRS):
            sl = slice(hp * LANES, (hp + 1) * LANES)
            kbuf_s[0, hp, 0:HIST, :] = kc[:, sl]
            vbuf_s[0, hp, 0:HIST, :] = vc[:, sl]
        st_s[...] = jnp.zeros_like(st_s)
        for h in range(N_HEADS):
            d0 = (h % 4) * HEAD_DIM
            st_s[h // 4, d0:d0 + HEAD_DIM, d0:d0 + HEAD_DIM] = s0_ref[0, h]
        xbuf[0:SUBLANES, :] = jnp.zeros((SUBLANES, W_BRANCH), F32)
        xbuf[SUBLANES - tail:SUBLANES, :] = c0_ref[0]
        h_s[...] = h0_ref[0]

    xb_s[...] = x_ref[0].astype(BF16)

    def proj(c):
        cols = slice(c * W_BRANCH, (c + 1) * W_BRANCH)
        return jnp.dot(xb_s[...], w_ref[:, cols], preferred_element_type=F32)

    def own_keys(c, out_ref, buf_s):
        full = proj(c)
        out_ref[0] = full
        half = full.astype(BF16)
        for hp in range(N_PAIRS):
            buf_s[own_slot, hp, own_rows, :] = half[:, hp * LANES:(hp + 1) * LANES]

    xr = proj(8)
    xbuf[SUBLANES:SUBLANES + T, :] = xr
    xc = cb_ref[...] + xr * cw_ref[tail:CONV_W, :]
    for j in range(tail):
        xc = xc + xbuf[SUBLANES - tail + j:SUBLANES - tail + j + T, :] * cw_ref[j:j + 1, :]
    xcb = xc.astype(BF16)
    def quad_dot(a, w_of):
        return jnp.concatenate([jnp.dot(a[:, g * QUAD:(g + 1) * QUAD], w_of(g), preferred_element_type=F32)
                                for g in range(N_QUADS)], axis=1)

    gate_a = quad_dot(xcb, lambda g: wga_ref[g])
    gate_x = quad_dot(xcb, lambda g: wgx_ref[g])
    qa = (proj(0) * (HEAD_DIM ** -0.5 * LOG2E)).astype(BF16)
    for hp in range(N_PAIRS):
        qm_s[2 * hp], qm_s[2 * hp + 1] = split_heads(qa[:, hp * LANES:(hp + 1) * LANES])
    r = _sigmoid(gate_a + bga_ref[...])
    ig = _sigmoid(gate_x + bgx_ref[...])
    log_a = (-LRU_C) * r * jax.nn.softplus(-lam_ref[...])
    a_t = jnp.exp(log_a)
    gap = 1.0 - a_t * a_t
    b_t = gap * lax.rsqrt(jnp.maximum(gap, SQRT_FLOOR)) * ig * xc
    n_groups = T // SUBLANES
    a3 = a_t.reshape(n_groups, SUBLANES, W_BRANCH)
    b3 = b_t.reshape(n_groups, SUBLANES, W_BRANCH)
    sub = lax.broadcasted_iota(jnp.int32, (n_groups, SUBLANES, W_BRANCH), 1)
    between_steps = {1: functools.partial(own_keys, 1, ko_ref, kbuf_s),
                     2: functools.partial(own_keys, 2, vo_ref, vbuf_s)}
    step = 1
    while step < SUBLANES:
        if step in between_steps:
            between_steps[step]()
        keep = sub >= step
        a_prev = jnp.where(keep, pltpu.roll(a3, step, 1), 1.0)
        b_prev = jnp.where(keep, pltpu.roll(b3, step, 1), 0.0)
        b3 = a3 * b_prev + b3
        a3 = a3 * a_prev
        step *= 2
    gated_c = _silu(proj(9))
    carry = h_s[...]
    for g in range(n_groups):
        hg = a3[g] * carry + b3[g]
        hseq_s[g * SUBLANES:(g + 1) * SUBLANES, :] = hg
        carry = hg[SUBLANES - 1:SUBLANES, :]
    h_s[...] = carry
    y_ref[0, :, 2 * W_BRANCH:3 * W_BRANCH] = (hseq_s[...] * gated_c).astype(BF16)
    xbuf[0:SUBLANES, :] = xbuf[T:T + SUBLANES, :]


    def attend(hist_tiles_present):
        if prompt:
            blocks = [(lax.rem(t + n_slots - back, n_slots), slice(0, T), slice(HIST - back * T, HIST - (back - 1) * T))
                      for back in range(hist_tiles_present, 0, -1)]
        else:
            blocks = [(0, slice(0, HIST), slice(0, HIST))]
        blocks.append((own_slot, own_rows, slice(HIST, HIST + T)))
        c0 = blocks[0][2].start

        def scores(h):
            for slot, rows, cols in blocks:
                sc_s[h % 2, :, cols] = (lax.dot_general(qm_s[h], kbuf_s[slot, h // 2, rows, :], _NT,
                                                        preferred_element_type=F32) + bias_ref[h, :, cols])

        def softmax(h):
            for r0 in range(0, T, SOFTMAX_ROWS):
                rows = slice(r0, r0 + SOFTMAX_ROWS)
                s = sc_s[h % 2, rows, c0:HIST + T]
                p = jnp.exp2(s - jnp.max(s, axis=1, keepdims=True))
                pr_s[h % 2, rows, c0:HIST + T] = p.astype(BF16)
                rl_s[h % 2, rows, :] = jnp.broadcast_to(1.0 / jnp.sum(p, axis=1, keepdims=True),
                                                        (SOFTMAX_ROWS, LANES))

        def weighted_values(h):
            sl = slice((h // 2) * LANES, (h // 2 + 1) * LANES)
            o = None
            for slot, rows, cols in blocks:
                part = jnp.dot(pr_s[h % 2, :, cols], vbuf_s[slot, h // 2, rows, :], preferred_element_type=F32)
                o = part if o is None else o + part
            o = o * rl_s[h % 2]
            ya_s[:, sl] = o if h % 2 == 0 else jnp.where(first_head, ya_s[:, sl], o)

        scores(0)
        for h in range(N_HEADS):
            if h + 1 < N_HEADS:
                scores(h + 1)
            softmax(h)
            weighted_values(h)

    if prompt:
        for k in range(hist_tiles):
            pl.when(t == k)(functools.partial(attend, k))
        pl.when(t >= hist_tiles)(functools.partial(attend, hist_tiles))
    else:
        attend(None)

    y_ref[0, :, 0:W_BRANCH] = (ya_s[...] * _silu(proj(3))).astype(BF16)

    cos = cos_ref[...]
    sin = sin_ref[...]
    lane_w = lax.broadcasted_iota(jnp.int32, (T, W_BRANCH), 1)
    low_half = (lane_w % HEAD_DIM) < (HEAD_DIM // 2)

    def rope(v):
        partner = jnp.where(low_half, pltpu.roll(v, W_BRANCH - HEAD_DIM // 2, 1), pltpu.roll(v, HEAD_DIM // 2, 1))
        return v * cos + partner * sin

    qr = rope(proj(4))
    kr = rope(proj(5)) * (HEAD_DIM ** -0.5)
    q_in = qr.astype(BF16)
    q_x = (qr * xi_ref[...]).astype(BF16)
    k_in = kr.astype(BF16)
    k_z = (kr * zeta_ref[...]).astype(BF16)
    v_b = proj(6).astype(BF16)
    gated_b = _silu(proj(7))
    for g in range(N_QUADS):
        gs = slice(g * QUAD, (g + 1) * QUAD)
        state = st_s[g]
        yb_s[:, gs] = jnp.dot(q_x[:, gs], state.astype(BF16), preferred_element_type=F32)
        outer = lax.dot_general(k_z[:, gs], v_b[:, gs], _TN, preferred_element_type=F32)
        st_s[g] = gmat_ref[g] * state + bmask_ref[...] * outer
    for hp in range(N_PAIRS):
        sl = slice(hp * LANES, (hp + 1) * LANES)
        k_heads = split_heads(k_in[:, sl])
        v_heads = split_heads(v_b[:, sl])
        acc = yb_s[:, sl]
        for e in range(2):
            h = 2 * hp + e
            qk = lax.dot_general(q_in[:, sl], k_heads[e], _NT, preferred_element_type=F32)
            inner = (qk * dec_ref[h]).astype(BF16)
            acc = acc + jnp.dot(inner, v_heads[e], preferred_element_type=F32)
        yb_s[:, sl] = acc

    def seg_mean(a):
        return quad_dot(a.astype(BF16), lambda g: seg_ref[...])

    yb = yb_s[...]
    dev = yb - seg_mean(yb)
    var = seg_mean(dev * dev)
    ybn = dev * lax.rsqrt(var + LN_EPS) * gn_ref[...]
    y_ref[0, :, W_BRANCH:2 * W_BRANCH] = (ybn * gated_b).astype(BF16)

    def write_state():
        for h in range(N_HEADS):
            d0 = (h % 4) * HEAD_DIM
            ret_ref[0, h] = st_s[h // 4, d0:d0 + HEAD_DIM, d0:d0 + HEAD_DIM]
        conv_ref[0] = xbuf[SUBLANES - tail:SUBLANES, :]
        lru_ref[0] = h_s[...]

    if prompt:
        pl.when(t == n_tiles - 1)(write_state)
    else:
        write_state()


def _mixer_call(layer, x, tables, w_in_b, bias, layer_w, cache, prev, *, T, prompt):
    B, S, D = x.shape
    depth = w_in_b.shape[0]
    n_tiles = S // T
    kv_rows = min(HIST, S)
    kv_first = n_tiles - kv_rows // T
    cos, sin, dec, xi, zeta, gmat, bmask, seg = tables
    gn, cw, cb, wga, bga, wgx, bgx, lam = layer_w

    in_specs = [
        pl.BlockSpec((1, T, D), lambda b, t: (b, t, 0)),
        pl.BlockSpec((T, W_BRANCH), lambda b, t: (t, 0)),
        pl.BlockSpec((T, W_BRANCH), lambda b, t: (t, 0)),
        _layer_spec(w_in_b, layer, block=(D, N_MIX_COLS)),
        _layer_spec(bias, layer),
        _const_spec(dec.shape), _const_spec(xi.shape), _const_spec(zeta.shape),
        _const_spec(gmat.shape), _const_spec(bmask.shape), _const_spec(seg.shape),
    ] + [_layer_spec(a, layer) for a in layer_w]
    args = [x, cos, sin, w_in_b, bias, dec, xi, zeta, gmat, bmask, seg, gn, cw, cb, wga, bga, wgx, bgx, lam]
    if not prompt:
        for a in cache:
            blk = (1,) + tuple(a.shape[2:])
            zeros = (0,) * (len(blk) - 1)
            in_specs.append(pl.BlockSpec((None,) + blk, lambda b, t, zeros=zeros: (layer, b) + zeros))
        args += list(cache)
    aliases = {}
    if prev is not None:
        for k, a in enumerate(prev):
            in_specs.append(pl.BlockSpec(memory_space=pl.ANY))
            aliases[len(args)] = k + 1
            args.append(a)

    def kv_map(b, t):
        return (layer, b, jnp.maximum(t - kv_first, 0), 0)

    out_specs = [
        pl.BlockSpec((1, T, 3 * W_BRANCH), lambda b, t: (b, t, 0)),
        pl.BlockSpec((None, 1, T, W_BRANCH), kv_map),
        pl.BlockSpec((None, 1, T, W_BRANCH), kv_map),
        pl.BlockSpec((None, 1, N_HEADS, HEAD_DIM, HEAD_DIM), lambda b, t: (layer, b, 0, 0, 0)),
        pl.BlockSpec((None, 1, CONV_W - 1, W_BRANCH), lambda b, t: (layer, b, 0, 0)),
        pl.BlockSpec((None, 1, 1, W_BRANCH), lambda b, t: (layer, b, 0, 0)),
    ]
    out_shape = [
        jax.ShapeDtypeStruct((B, S, 3 * W_BRANCH), BF16),
        jax.ShapeDtypeStruct((depth, B, kv_rows, W_BRANCH), F32),
        jax.ShapeDtypeStruct((depth, B, kv_rows, W_BRANCH), F32),
        jax.ShapeDtypeStruct((depth, B, N_HEADS, HEAD_DIM, HEAD_DIM), F32),
        jax.ShapeDtypeStruct((depth, B, CONV_W - 1, W_BRANCH), F32),
        jax.ShapeDtypeStruct((depth, B, 1, W_BRANCH), F32),
    ]
    kv_buf_shape = (HIST // T + 1, N_PAIRS, T, LANES) if prompt else (1, N_PAIRS, HIST + T, LANES)
    scratch = [
        pltpu.VMEM((T, D), BF16),
        pltpu.VMEM((N_HEADS, T, LANES), BF16),
        pltpu.VMEM(kv_buf_shape, BF16),
        pltpu.VMEM(kv_buf_shape, BF16),
        pltpu.VMEM((T, W_BRANCH), F32),
        pltpu.VMEM((T, W_BRANCH), F32),
        pltpu.VMEM((N_QUADS, QUAD, QUAD), F32),
        pltpu.VMEM((T + SUBLANES, W_BRANCH), F32),
        pltpu.VMEM((T, W_BRANCH), F32),
        pltpu.VMEM((1, W_BRANCH), F32),
        pltpu.VMEM((2, T, HIST + T), F32),
        pltpu.VMEM((2, T, HIST + T), BF16),
        pltpu.VMEM((2, T, LANES), F32),
    ]
    return pl.pallas_call(
        functools.partial(_mixer_kernel, T=T, prompt=prompt, n_tiles=n_tiles, n_alias=len(aliases)),
        grid=(B, n_tiles),
        in_specs=in_specs,
        out_specs=out_specs,
        out_shape=out_shape,
        scratch_shapes=scratch,
        input_output_aliases=aliases,
        compiler_params=pltpu.CompilerParams(
            dimension_semantics=("arbitrary", "arbitrary"), vmem_limit_bytes=VMEM_LIMIT_BYTES,
            ),
        name="mixer_prompt" if prompt else "mixer_sample",
    )(*args)


def _merge_kernel(x_ref, y_ref, p_ref, wg0_ref, wg1_ref, wg2_ref, wb_ref, wo_ref, wpg_ref, wpe_ref, lng_ref,
                  lnb_ref, o_ref, m_s, r_s, *, alpha, rows_per_pass):
    passes = [slice(r0, r0 + rows_per_pass) for r0 in range(0, x_ref.shape[0], rows_per_pass)]

    def branch_merge(rows):
        xb = x_ref[rows, :].astype(BF16)
        merged = None
        for b, wg_ref in enumerate((wg0_ref, wg1_ref, wg2_ref)):
            gate = _sigmoid(jnp.dot(xb, wg_ref[...], preferred_element_type=F32))
            term = gate * jnp.dot(y_ref[rows, b * W_BRANCH:(b + 1) * W_BRANCH], wb_ref[b],
                                  preferred_element_type=F32)
            merged = term if merged is None else merged + term
        m_s[rows, :] = merged.astype(BF16)

    def residual(rows):
        r = alpha * x_ref[rows, :] + jnp.dot(m_s[rows, :], wo_ref[...], preferred_element_type=F32)
        pgate = _sigmoid(jnp.dot(r.astype(BF16), wpg_ref[...], preferred_element_type=F32))
        r_s[rows, :] = r + pgate * jnp.dot(p_ref[rows, :].astype(BF16), wpe_ref[...], preferred_element_type=F32)

    def layer_norm(rows):
        r = r_s[rows, :]
        mu = jnp.mean(r, axis=-1, keepdims=True)
        dev = r - mu
        var = jnp.mean(dev * dev, axis=-1, keepdims=True)
        o_ref[rows, :] = dev * lax.rsqrt(var + LN_EPS) * lng_ref[...] + lnb_ref[...]

    branch_merge(passes[0])
    residual(passes[0])
    for prev, rows in zip(passes[:-1], passes[1:]):
        branch_merge(rows)
        layer_norm(prev)
        residual(rows)
    layer_norm(passes[-1])


def _merge_call(layer, x2, y2, p3, w_in_b, merge_w, *, alpha):
    n, d = x2.shape
    tm = min(MERGE_TILE, n)
    rows_per_pass = min(MERGE_ROWS, tm)
    gate_block0 = N_MIX_COLS // d
    return pl.pallas_call(
        functools.partial(_merge_kernel, alpha=alpha, rows_per_pass=rows_per_pass),
        grid=(n // tm,),
        scratch_shapes=[pltpu.VMEM((tm, d), BF16),
                        pltpu.VMEM((tm, d), F32)],
        in_specs=[
            pl.BlockSpec((tm, d), lambda i: (i, 0)),
            pl.BlockSpec((tm, y2.shape[1]), lambda i: (i, 0)),
            pl.BlockSpec((None, tm, p3.shape[2]), lambda i: (layer, i, 0)),
        ] + [_layer_spec(w_in_b, layer, block=(d, d), index=(0, gate_block0 + b)) for b in range(3)]
          + [_layer_spec(a, layer) for a in merge_w],
        out_specs=pl.BlockSpec((tm, d), lambda i: (i, 0)),
        out_shape=jax.ShapeDtypeStruct((n, d), F32),
        compiler_params=pltpu.CompilerParams(
            dimension_semantics=("arbitrary",), vmem_limit_bytes=VMEM_LIMIT_BYTES),
        name="merge",
    )(x2, y2, p3, w_in_b, w_in_b, w_in_b, *merge_w)


def _tables(pos, blk):
    half = HEAD_DIM // 2
    inv = ROPE_BASE ** (-np.arange(half, dtype=np.float64) / half)
    ang = np.asarray(pos, np.float64)[:, None] * inv[None, :]
    c, s = np.cos(ang), np.sin(ang)
    cos = np.tile(np.concatenate([c, c], axis=-1), (1, N_HEADS))
    sin = np.tile(np.concatenate([-s, s], axis=-1), (1, N_HEADS))

    log_g = np.log1p(-np.exp2(-5.0 - np.arange(N_HEADS, dtype=np.float64)))
    i = np.arange(blk, dtype=np.float64)
    diff = i[:, None] - i[None, :]
    decay = np.where(diff >= 0, np.exp(log_g[:, None, None] * np.maximum(diff, 0.0)), 0.0)
    zeta = np.exp(log_g[:, None] * (blk - 1.0 - i)[None, :])
    xi = np.exp(log_g[:, None] * (i + 1.0)[None, :])
    g_blk = np.exp(log_g * blk)
    xi_full = np.repeat(xi.T, HEAD_DIM, axis=1)
    zeta_full = np.repeat(zeta.T, HEAD_DIM, axis=1)
    q = np.arange(QUAD)
    bmask = (q[:, None] // HEAD_DIM == q[None, :] // HEAD_DIM).astype(np.float64)
    gmat = np.stack([bmask * np.repeat(g_blk[4 * g:4 * g + 4], HEAD_DIM)[None, :] for g in range(N_QUADS)])
    seg = bmask / HEAD_DIM
    f32 = lambda a: jnp.asarray(a, F32)
    return (f32(cos), f32(sin), f32(decay), f32(xi_full), f32(zeta_full), f32(gmat), f32(bmask),
            jnp.asarray(seg, BF16))


def _block_diag_quads(w):
    depth, n, bi, bj = w.shape
    per = QUAD // bi
    groups = w.reshape(depth * (n // per), per * bi, bj)
    tiled = jnp.tile(groups, (1, 1, per))
    r = np.arange(per * bi)[:, None] // bi
    c = np.arange(per * bj)[None, :] // bj
    return jnp.where(jnp.asarray(r == c)[None], tiled, 0.0).reshape(depth, n // per, per * bi, per * bj)


def kernel(x_prompt, x_sample, p_prompt, p_sample, cache_k_a, cache_v_a, state_ret, state_conv, state_lru, w_in, rel_table, gn_gain, conv_w, conv_b, w_gate_a, b_gate_a, w_gate_x, b_gate_x, lru_lambda, w_branch, w_out, ln_gain, ln_bias, w_ple, w_ple_gate):
    depth = w_in.shape[0]
    bp, sp, d_model = x_prompt.shape
    bs, ss, _ = x_sample.shape
    assert cache_k_a.shape[2] == HIST and w_in.shape[2] == N_MIX_COLS + 3 * d_model
    assert N_MIX_COLS % d_model == 0
    tp = min(PROMPT_TILE, sp)
    assert sp % tp == 0 and tp % LANES == 0 and HIST % tp == 0 and ss % SUBLANES == 0
    alpha = float((2 * depth) ** 0.25)

    w_in_b = w_in.astype(BF16)
    row = lambda a: a.reshape(depth, 1, a.shape[-1])
    layer_w = (row(gn_gain), conv_w, row(conv_b), _block_diag_quads(w_gate_a).astype(BF16), row(b_gate_a),
               _block_diag_quads(w_gate_x).astype(BF16), row(b_gate_x), row(lru_lambda))
    merge_w = (w_branch.astype(BF16), w_out.astype(BF16), w_ple_gate.astype(BF16), w_ple.astype(BF16),
               row(ln_gain), row(ln_bias))

    bias_p, bias_s = _expand_bias(rel_table, tp, ss)
    tables_p = _tables(np.arange(sp), tp)
    tables_s = _tables(PAST_LEN + np.arange(ss), ss)
    cache = (cache_k_a.reshape(depth, bs, HIST, W_BRANCH), cache_v_a.reshape(depth, bs, HIST, W_BRANCH),
             state_ret, state_conv, state_lru.reshape(depth, bs, 1, W_BRANCH))
    pp = p_prompt.reshape(depth, bp * sp, -1)
    ps = p_sample.reshape(depth, bs * ss, -1)

    hp, hs = x_prompt, x_sample
    outs_p = outs_s = None
    for l in range(depth):
        y, *outs_p = _mixer_call(l, hp, tables_p, w_in_b, bias_p, layer_w, None, outs_p, T=tp, prompt=True)
        hp = _merge_call(l, hp.reshape(bp * sp, d_model), y.reshape(bp * sp, -1), pp, w_in_b, merge_w,
                         alpha=alpha).reshape(bp, sp, d_model)
        y, *outs_s = _mixer_call(l, hs, tables_s, w_in_b, bias_s, layer_w, cache, outs_s, T=ss, prompt=False)
        hs = _merge_call(l, hs.reshape(bs * ss, d_model), y.reshape(bs * ss, -1), ps, w_in_b, merge_w,
                         alpha=alpha).reshape(bs, ss, d_model)

    k_p, v_p, r_p, c_p, s_p = outs_p
    k_s, v_s, r_s, c_s, s_s = outs_s
    heads = lambda a: a.reshape(a.shape[:-1] + (N_HEADS, HEAD_DIM))
    return (hp, hs, heads(k_p), heads(v_p), heads(k_s), heads(v_s), r_p, r_s, c_p, c_s,
            s_p.reshape(depth, bp, W_BRANCH), s_s.reshape(depth, bs, W_BRANCH))
```

```python
import functools

import numpy as np

import jax
import jax.numpy as jnp
from jax import lax
from jax.experimental import pallas as pl
from jax.experimental.pallas import tpu as pltpu

F32 = jnp.float32
BF16 = jnp.bfloat16

CHUNK = 64
N_LEFT_CHUNKS = 8
HIST = N_LEFT_CHUNKS * CHUNK
HEAD_DIM = 64
N_HEADS = 8
W_BRANCH = N_HEADS * HEAD_DIM
N_PAIRS = N_HEADS // 2
LANES = 128
SUBLANES = 8
QUAD = 4 * HEAD_DIM
N_QUADS = N_HEADS // 4
REL_CLIP = 128
CONV_W = 4
LRU_C = 8.0
ROPE_BASE = 10000.0
LN_EPS = 1e-5
NEG_INF = -1e30
LOG2E = 1.4426950408889634
PAST_LEN = 1024
N_MIX_COLS = 10 * W_BRANCH
PROMPT_TILE = 256
MERGE_TILE = 1024
MERGE_ROWS = 256
SOFTMAX_ROWS = 16
SQRT_FLOOR = 1e-30
VMEM_LIMIT_BYTES = 56 * 1024 * 1024

_NT = (((1,), (1,)), ((), ()))
_TN = (((0,), (0,)), ((), ()))


def _const_spec(shape):
    zeros = (0,) * len(shape)
    return pl.BlockSpec(shape, lambda *_: zeros, pipeline_mode=pl.Buffered(1))


def _layer_spec(arr, layer, block=None, index=None):
    block = tuple(arr.shape[1:]) if block is None else tuple(block)
    index = (0,) * len(block) if index is None else tuple(index)
    return pl.BlockSpec((None,) + block, lambda *_: (layer,) + index, pipeline_mode=pl.Buffered(1))


def _sigmoid(x):
    return 0.5 * jnp.tanh(0.5 * x) + 0.5


def _silu(x):
    half = 0.5 * x
    return half * jnp.tanh(half) + half


def _bias_kernel(tab_ref, bp_ref, bs_ref, *, tq, ts):
    h = pl.program_id(0)

    blocks = {}

    def toeplitz_block(rows, i0, j0):
        i = lax.broadcasted_iota(jnp.int32, (rows, LANES), 0) + i0
        j = lax.broadcasted_iota(jnp.int32, (rows, LANES), 1) + j0
        if (rows, i0 - j0) in blocks:
            return blocks[rows, i0 - j0], i, j
        idx = jnp.clip(HIST + i - j, -REL_CLIP, REL_CLIP) + REL_CLIP
        d_lo = HIST + i0 - (j0 + LANES - 1)
        d_hi = HIST + i0 + rows - 1 - j0
        k_lo = min(max(d_lo, -REL_CLIP), REL_CLIP) + REL_CLIP
        k_hi = min(max(d_hi, -REL_CLIP), REL_CLIP) + REL_CLIP

        def body(k, acc):
            return jnp.where(idx == k, tab_ref[h, k], acc)

        blk = lax.fori_loop(k_lo, k_hi + 1, body, jnp.zeros((rows, LANES), F32), unroll=8) * LOG2E
        blocks[rows, i0 - j0] = blk
        return blk, i, j

    for ib in range(tq // LANES):
        for jb in range((HIST + tq) // LANES):
            blk, i, j = toeplitz_block(LANES, ib * LANES, jb * LANES)
            qc = i // CHUNK
            kc = j // CHUNK
            vis = (kc >= qc) & (kc <= qc + N_LEFT_CHUNKS)
            bp_ref[0, ib * LANES:(ib + 1) * LANES, jb * LANES:(jb + 1) * LANES] = jnp.where(vis, blk, NEG_INF)
    for jb in range(bs_ref.shape[2] // LANES):
        blk, _, _ = toeplitz_block(ts, 0, jb * LANES)
        bs_ref[0, :, jb * LANES:(jb + 1) * LANES] = blk


def _expand_bias(rel_table, tq, ts):
    depth, nh, nrel = rel_table.shape
    ws = -(-(HIST + ts) // LANES) * LANES
    bp, bs = pl.pallas_call(
        functools.partial(_bias_kernel, tq=tq, ts=ts),
        grid=(depth * nh,),
        in_specs=[pl.BlockSpec(memory_space=pltpu.SMEM)],
        out_specs=[pl.BlockSpec((1, tq, HIST + tq), lambda g: (g, 0, 0)),
                   pl.BlockSpec((1, ts, ws), lambda g: (g, 0, 0))],
        out_shape=[jax.ShapeDtypeStruct((depth * nh, tq, HIST + tq), F32),
                   jax.ShapeDtypeStruct((depth * nh, ts, ws), F32)],
        name="rel_bias",
    )(rel_table.reshape(depth * nh, nrel))
    return bp.reshape(depth, nh, tq, HIST + tq), bs.reshape(depth, nh, ts, ws)


def _mixer_kernel(*refs, T, prompt, n_tiles, n_alias):
    n_in = 19 if prompt else 24
    (x_ref, cos_ref, sin_ref, w_ref, bias_ref, dec_ref, xi_ref, zeta_ref, gmat_ref, bmask_ref, seg_ref, gn_ref,
     cw_ref, cb_ref, wga_ref, bga_ref, wgx_ref, bgx_ref, lam_ref) = refs[:19]
    if not prompt:
        kc_ref, vc_ref, s0_ref, c0_ref, h0_ref = refs[19:24]
    refs = refs[n_in + n_alias:]
    (y_ref, ko_ref, vo_ref, ret_ref, conv_ref, lru_ref,
     xb_s, qm_s, kbuf_s, vbuf_s, ya_s, yb_s, st_s, xbuf, hseq_s, h_s, sc_s, pr_s, rl_s) = refs

    t = pl.program_id(1)
    tail = CONV_W - 1
    lane = lax.broadcasted_iota(jnp.int32, (T, LANES), 1)
    first_head = lane < HEAD_DIM

    def split_heads(pair):
        zero = jnp.zeros_like(pair)
        return jnp.where(first_head, pair, zero), jnp.where(first_head, zero, pair)

    hist_tiles = HIST // T
    n_slots = hist_tiles + 1
    own_slot = lax.rem(t, n_slots) if prompt else 0
    own_rows = slice(0, T) if prompt else slice(HIST, HIST + T)

    if prompt:
        @pl.when(t == 0)
        def _():
            st_s[...] = jnp.zeros_like(st_s)
            xbuf[0:SUBLANES, :] = jnp.zeros((SUBLANES, W_BRANCH), F32)
            h_s[...] = jnp.zeros_like(h_s)
    else:
        kc = kc_ref[0].astype(BF16)
        vc = vc_ref[0].astype(BF16)
        for hp in range(N_PAIRS):
            sl = slice(hp * LANES, (hp + 1) * LANES)
            kbuf_s[0, hp, 0:HIST, :] = kc[:, sl]
            vbuf_s[0, hp, 0:HIST, :] = vc[:, sl]
        st_s[...] = jnp.zeros_like(st_s)
        for h in range(N_HEADS):
            d0 = (h % 4) * HEAD_DIM
            st_s[h // 4, d0:d0 + HEAD_DIM, d0:d0 + HEAD_DIM] = s0_ref[0, h]
        xbuf[0:SUBLANES, :] = jnp.zeros((SUBLANES, W_BRANCH), F32)
        xbuf[SUBLANES - tail:SUBLANES, :] = c0_ref[0]
        h_s[...] = h0_ref[0]

    xb_s[...] = x_ref[0].astype(BF16)

    def proj(c):
        cols = slice(c * W_BRANCH, (c + 1) * W_BRANCH)
        return jnp.dot(xb_s[...], w_ref[:, cols], preferred_element_type=F32)

    def quad_dot(a, w_of):
        return jnp.concatenate([jnp.dot(a[:, g * QUAD:(g + 1) * QUAD], w_of(g), preferred_element_type=F32)
                                for g in range(N_QUADS)], axis=1)

    cos = cos_ref[...]
    sin = sin_ref[...]
    lane_w = lax.broadcasted_iota(jnp.int32, (T, W_BRANCH), 1)
    low_half = (lane_w % HEAD_DIM) < (HEAD_DIM // 2)

    def rope(v):
        partner = jnp.where(low_half, pltpu.roll(v, W_BRANCH - HEAD_DIM // 2, 1), pltpu.roll(v, HEAD_DIM // 2, 1))
        return v * cos + partner * sin

    qr = rope(proj(4))
    kr = rope(proj(5)) * (HEAD_DIM ** -0.5)
    q_in = qr.astype(BF16)
    q_x = (qr * xi_ref[...]).astype(BF16)
    k_in = kr.astype(BF16)
    k_z = (kr * zeta_ref[...]).astype(BF16)
    v_b = proj(6).astype(BF16)
    gated_b = _silu(proj(7))
    for g in range(N_QUADS):
        gs = slice(g * QUAD, (g + 1) * QUAD)
        state = st_s[g]
        yb_s[:, gs] = jnp.dot(q_x[:, gs], state.astype(BF16), preferred_element_type=F32)
        outer = lax.dot_general(k_z[:, gs], v_b[:, gs], _TN, preferred_element_type=F32)
        st_s[g] = gmat_ref[g] * state + bmask_ref[...] * outer
    for hp in range(N_PAIRS):
        sl = slice(hp * LANES, (hp + 1) * LANES)
        k_heads = split_heads(k_in[:, sl])
        v_heads = split_heads(v_b[:, sl])
        acc = yb_s[:, sl]
        for e in range(2):
            h = 2 * hp + e
            qk = lax.dot_general(q_in[:, sl], k_heads[e], _NT, preferred_element_type=F32)
            inner = (qk * dec_ref[h]).astype(BF16)
            acc = acc + jnp.dot(inner, v_heads[e], preferred_element_type=F32)
        yb_s[:, sl] = acc

    def seg_mean(a):
        return quad_dot(a.astype(BF16), lambda g: seg_ref[...])

    yb = yb_s[...]
    dev = yb - seg_mean(yb)
    var = seg_mean(dev * dev)
    ybn = dev * lax.rsqrt(var + LN_EPS) * gn_ref[...]
    y_ref[0, :, W_BRANCH:2 * W_BRANCH] = (ybn * gated_b).astype(BF16)

    def own_keys(c, out_ref, buf_s):
        full = proj(c)
        out_ref[0] = full
        half = full.astype(BF16)
        for hp in range(N_PAIRS):
            buf_s[own_slot, hp, own_rows, :] = half[:, hp * LANES:(hp + 1) * LANES]

    xr = proj(8)
    xbuf[SUBLANES:SUBLANES + T, :] = xr
    xc = cb_ref[...] + xr * cw_ref[tail:CONV_W, :]
    for j in range(tail):
        xc = xc + xbuf[SUBLANES - tail + j:SUBLANES - tail + j + T, :] * cw_ref[j:j + 1, :]
    xcb = xc.astype(BF16)
    gate_a = quad_dot(xcb, lambda g: wga_ref[g])
    gate_x = quad_dot(xcb, lambda g: wgx_ref[g])
    qa = (proj(0) * (HEAD_DIM ** -0.5 * LOG2E)).astype(BF16)
    for hp in range(N_PAIRS):
        qm_s[2 * hp], qm_s[2 * hp + 1] = split_heads(qa[:, hp * LANES:(hp + 1) * LANES])
    r = _sigmoid(gate_a + bga_ref[...])
    ig = _sigmoid(gate_x + bgx_ref[...])
    log_a = (-LRU_C) * r * jax.nn.softplus(-lam_ref[...])
    a_t = jnp.exp(log_a)
    gap = 1.0 - a_t * a_t
    b_t = gap * lax.rsqrt(jnp.maximum(gap, SQRT_FLOOR)) * ig * xc
    n_groups = T // SUBLANES
    a3 = a_t.reshape(n_groups, SUBLANES, W_BRANCH)
    b3 = b_t.reshape(n_groups, SUBLANES, W_BRANCH)
    sub = lax.broadcasted_iota(jnp.int32, (n_groups, SUBLANES, W_BRANCH), 1)
    step = 1
    while step < SUBLANES:
        if step == 2:
            own_keys(1, ko_ref, kbuf_s)
        keep = sub >= step
        a_prev = jnp.where(keep, pltpu.roll(a3, step, 1), 1.0)
        b_prev = jnp.where(keep, pltpu.roll(b3, step, 1), 0.0)
        b3 = a3 * b_prev + b3
        a3 = a3 * a_prev
        step *= 2
    carry = h_s[...]
    for g in range(n_groups):
        hg = a3[g] * carry + b3[g]
        hseq_s[g * SUBLANES:(g + 1) * SUBLANES, :] = hg
        carry = hg[SUBLANES - 1:SUBLANES, :]
    h_s[...] = carry
    own_keys(2, vo_ref, vbuf_s)
    gated_c = _silu(proj(9))
    y_ref[0, :, 2 * W_BRANCH:3 * W_BRANCH] = (hseq_s[...] * gated_c).astype(BF16)
    xbuf[0:SUBLANES, :] = xbuf[T:T + SUBLANES, :]


    def attend(hist_tiles_present):
        if prompt:
            blocks = [(lax.rem(t + n_slots - back, n_slots), slice(0, T), slice(HIST - back * T, HIST - (back - 1) * T))
                      for back in range(hist_tiles_present, 0, -1)]
        else:
            blocks = [(0, slice(0, HIST), slice(0, HIST))]
        blocks.append((own_slot, own_rows, slice(HIST, HIST + T)))
        c0 = blocks[0][2].start

        def scores(h):
            for slot, rows, cols in blocks:
                sc_s[h % 2, :, cols] = (lax.dot_general(qm_s[h], kbuf_s[slot, h // 2, rows, :], _NT,
                                                        preferred_element_type=F32) + bias_ref[h, :, cols])

        def softmax(h):
            for r0 in range(0, T, SOFTMAX_ROWS):
                rows = slice(r0, r0 + SOFTMAX_ROWS)
                s = sc_s[h % 2, rows, c0:HIST + T]
                p = jnp.exp2(s - jnp.max(s, axis=1, keepdims=True))
                pr_s[h % 2, rows, c0:HIST + T] = p.astype(BF16)
                rl_s[h % 2, rows, :] = jnp.broadcast_to(1.0 / jnp.sum(p, axis=1, keepdims=True),
                                                        (SOFTMAX_ROWS, LANES))

        def weighted_values(h):
            sl = slice((h // 2) * LANES, (h // 2 + 1) * LANES)
            o = None
            for slot, rows, cols in blocks:
                part = jnp.dot(pr_s[h % 2, :, cols], vbuf_s[slot, h // 2, rows, :], preferred_element_type=F32)
                o = part if o is None else o + part
            o = o * rl_s[h % 2]
            ya_s[:, sl] = o if h % 2 == 0 else jnp.where(first_head, ya_s[:, sl], o)

        scores(0)
        for h in range(N_HEADS):
            if h + 1 < N_HEADS:
                scores(h + 1)
            else:
                gated_a = _silu(proj(3))
            softmax(h)
            weighted_values(h)
        y_ref[0, :, 0:W_BRANCH] = (ya_s[...] * gated_a).astype(BF16)

    if prompt:
        for k in range(hist_tiles):
            pl.when(t == k)(functools.partial(attend, k))
        pl.when(t >= hist_tiles)(functools.partial(attend, hist_tiles))
    else:
        attend(None)

    def write_state():
        for h in range(N_HEADS):
            d0 = (h % 4) * HEAD_DIM
            ret_ref[0, h] = st_s[h // 4, d0:d0 + HEAD_DIM, d0:d0 + HEAD_DIM]
        conv_ref[0] = xbuf[SUBLANES - tail:SUBLANES, :]
        lru_ref[0] = h_s[...]

    if prompt:
        pl.when(t == n_tiles - 1)(write_state)
    else:
        write_state()


def _mixer_call(layer, x, tables, w_in_b, bias, layer_w, cache, prev, *, T, prompt):
    B, S, D = x.shape
    depth = w_in_b.shape[0]
    n_tiles = S // T
    kv_rows = min(HIST, S)
    kv_first = n_tiles - kv_rows // T
    cos, sin, dec, xi, zeta, gmat, bmask, seg = tables
    gn, cw, cb, wga, bga, wgx, bgx, lam = layer_w

    in_specs = [
        pl.BlockSpec((1, T, D), lambda b, t: (b, t, 0)),
        pl.BlockSpec((T, W_BRANCH), lambda b, t: (t, 0)),
        pl.BlockSpec((T, W_BRANCH), lambda b, t: (t, 0)),
        _layer_spec(w_in_b, layer, block=(D, N_MIX_COLS)),
        _layer_spec(bias, layer),
        _const_spec(dec.shape), _const_spec(xi.shape), _const_spec(zeta.shape),
        _const_spec(gmat.shape), _const_spec(bmask.shape), _const_spec(seg.shape),
    ] + [_layer_spec(a, layer) for a in layer_w]
    args = [x, cos, sin, w_in_b, bias, dec, xi, zeta, gmat, bmask, seg, gn, cw, cb, wga, bga, wgx, bgx, lam]
    if not prompt:
        for a in cache:
            blk = (1,) + tuple(a.shape[2:])
            zeros = (0,) * (len(blk) - 1)
            in_specs.append(pl.BlockSpec((None,) + blk, lambda b, t, zeros=zeros: (layer, b) + zeros))
        args += list(cache)
    aliases = {}
    if prev is not None:
        for k, a in enumerate(prev):
            in_specs.append(pl.BlockSpec(memory_space=pl.ANY))
            aliases[len(args)] = k + 1
            args.append(a)

    def kv_map(b, t):
        return (layer, b, jnp.maximum(t - kv_first, 0), 0)

    out_specs = [
        pl.BlockSpec((1, T, 3 * W_BRANCH), lambda b, t: (b, t, 0)),
        pl.BlockSpec((None, 1, T, W_BRANCH), kv_map),
        pl.BlockSpec((None, 1, T, W_BRANCH), kv_map),
        pl.BlockSpec((None, 1, N_HEADS, HEAD_DIM, HEAD_DIM), lambda b, t: (layer, b, 0, 0, 0)),
        pl.BlockSpec((None, 1, CONV_W - 1, W_BRANCH), lambda b, t: (layer, b, 0, 0)),
        pl.BlockSpec((None, 1, 1, W_BRANCH), lambda b, t: (layer, b, 0, 0)),
    ]
    out_shape = [
        jax.ShapeDtypeStruct((B, S, 3 * W_BRANCH), BF16),
        jax.ShapeDtypeStruct((depth, B, kv_rows, W_BRANCH), F32),
        jax.ShapeDtypeStruct((depth, B, kv_rows, W_BRANCH), F32),
        jax.ShapeDtypeStruct((depth, B, N_HEADS, HEAD_DIM, HEAD_DIM), F32),
        jax.ShapeDtypeStruct((depth, B, CONV_W - 1, W_BRANCH), F32),
        jax.ShapeDtypeStruct((depth, B, 1, W_BRANCH), F32),
    ]
    kv_buf_shape = (HIST // T + 1, N_PAIRS, T, LANES) if prompt else (1, N_PAIRS, HIST + T, LANES)
    scratch = [
        pltpu.VMEM((T, D), BF16),
        pltpu.VMEM((N_HEADS, T, LANES), BF16),
        pltpu.VMEM(kv_buf_shape, BF16),
        pltpu.VMEM(kv_buf_shape, BF16),
        pltpu.VMEM((T, W_BRANCH), F32),
        pltpu.VMEM((T, W_BRANCH), F32),
        pltpu.VMEM((N_QUADS, QUAD, QUAD), F32),
        pltpu.VMEM((T + SUBLANES, W_BRANCH), F32),
        pltpu.VMEM((T, W_BRANCH), F32),
        pltpu.VMEM((1, W_BRANCH), F32),
        pltpu.VMEM((2, T, HIST + T), F32),
        pltpu.VMEM((2, T, HIST + T), BF16),
        pltpu.VMEM((2, T, LANES), F32),
    ]
    return pl.pallas_call(
        functools.partial(_mixer_kernel, T=T, prompt=prompt, n_tiles=n_tiles, n_alias=len(aliases)),
        grid=(B, n_tiles),
        in_specs=in_specs,
        out_specs=out_specs,
        out_shape=out_shape,
        scratch_shapes=scratch,
        input_output_aliases=aliases,
        compiler_params=pltpu.CompilerParams(
            dimension_semantics=("arbitrary", "arbitrary"), vmem_limit_bytes=VMEM_LIMIT_BYTES,
            ),
        name="mixer_prompt" if prompt else "mixer_sample",
    )(*args)


def _merge_kernel(x_ref, y_ref, p_ref, wg0_ref, wg1_ref, wg2_ref, wb_ref, wo_ref, wpg_ref, wpe_ref, lng_ref,
                  lnb_ref, o_ref, m_s, r_s, *, alpha, rows_per_pass):
    passes = [slice(r0, r0 + rows_per_pass) for r0 in range(0, x_ref.shape[0], rows_per_pass)]

    def branch_merge(rows):
        xb = x_ref[rows, :].astype(BF16)
        merged = None
        for b, wg_ref in enumerate((wg0_ref, wg1_ref, wg2_ref)):
            gate = _sigmoid(jnp.dot(xb, wg_ref[...], preferred_element_type=F32))
            term = gate * jnp.dot(y_ref[rows, b * W_BRANCH:(b + 1) * W_BRANCH], wb_ref[b],
                                  preferred_element_type=F32)
            merged = term if merged is None else merged + term
        m_s[rows, :] = merged.astype(BF16)

    def residual(rows):
        r = alpha * x_ref[rows, :] + jnp.dot(m_s[rows, :], wo_ref[...], preferred_element_type=F32)
        pgate = _sigmoid(jnp.dot(r.astype(BF16), wpg_ref[...], preferred_element_type=F32))
        r_s[rows, :] = r + pgate * jnp.dot(p_ref[rows, :].astype(BF16), wpe_ref[...], preferred_element_type=F32)

    def layer_norm(rows):
        r = r_s[rows, :]
        mu = jnp.mean(r, axis=-1, keepdims=True)
        dev = r - mu
        var = jnp.mean(dev * dev, axis=-1, keepdims=True)
        o_ref[rows, :] = dev * lax.rsqrt(var + LN_EPS) * lng_ref[...] + lnb_ref[...]

    branch_merge(passes[0])
    residual(passes[0])
    for prev, rows in zip(passes[:-1], passes[1:]):
        branch_merge(rows)
        layer_norm(prev)
        residual(rows)
    layer_norm(passes[-1])


def _merge_call(layer, x2, y2, p3, w_in_b, merge_w, *, alpha):
    n, d = x2.shape
    tm = min(MERGE_TILE, n)
    rows_per_pass = min(MERGE_ROWS, tm)
    gate_block0 = N_MIX_COLS // d
    return pl.pallas_call(
        functools.partial(_merge_kernel, alpha=alpha, rows_per_pass=rows_per_pass),
        grid=(n // tm,),
        scratch_shapes=[pltpu.VMEM((tm, d), BF16),
                        pltpu.VMEM((tm, d), F32)],
        in_specs=[
            pl.BlockSpec((tm, d), lambda i: (i, 0)),
            pl.BlockSpec((tm, y2.shape[1]), lambda i: (i, 0)),
            pl.BlockSpec((None, tm, p3.shape[2]), lambda i: (layer, i, 0)),
        ] + [_layer_spec(w_in_b, layer, block=(d, d), index=(0, gate_block0 + b)) for b in range(3)]
          + [_layer_spec(a, layer) for a in merge_w],
        out_specs=pl.BlockSpec((tm, d), lambda i: (i, 0)),
        out_shape=jax.ShapeDtypeStruct((n, d), F32),
        compiler_params=pltpu.CompilerParams(
            dimension_semantics=("arbitrary",), vmem_limit_bytes=VMEM_LIMIT_BYTES),
        name="merge",
    )(x2, y2, p3, w_in_b, w_in_b, w_in_b, *merge_w)


def _tables(pos, blk):
    half = HEAD_DIM // 2
    inv = ROPE_BASE ** (-np.arange(half, dtype=np.float64) / half)
    ang = np.asarray(pos, np.float64)[:, None] * inv[None, :]
    c, s = np.cos(ang), np.sin(ang)
    cos = np.tile(np.concatenate([c, c], axis=-1), (1, N_HEADS))
    sin = np.tile(np.concatenate([-s, s], axis=-1), (1, N_HEADS))

    log_g = np.log1p(-np.exp2(-5.0 - np.arange(N_HEADS, dtype=np.float64)))
    i = np.arange(blk, dtype=np.float64)
    diff = i[:, None] - i[None, :]
    decay = np.where(diff >= 0, np.exp(log_g[:, None, None] * np.maximum(diff, 0.0)), 0.0)
    zeta = np.exp(log_g[:, None] * (blk - 1.0 - i)[None, :])
    xi = np.exp(log_g[:, None] * (i + 1.0)[None, :])
    g_blk = np.exp(log_g * blk)
    xi_full = np.repeat(xi.T, HEAD_DIM, axis=1)
    zeta_full = np.repeat(zeta.T, HEAD_DIM, axis=1)
    q = np.arange(QUAD)
    bmask = (q[:, None] // HEAD_DIM == q[None, :] // HEAD_DIM).astype(np.float64)
    gmat = np.stack([bmask * np.repeat(g_blk[4 * g:4 * g + 4], HEAD_DIM)[None, :] for g in range(N_QUADS)])
    seg = bmask / HEAD_DIM
    f32 = lambda a: jnp.asarray(a, F32)
    return (f32(cos), f32(sin), f32(decay), f32(xi_full), f32(zeta_full), f32(gmat), f32(bmask),
            jnp.asarray(seg, BF16))


def _block_diag_quads(w):
    depth, n, bi, bj = w.shape
    per = QUAD // bi
    groups = w.reshape(depth * (n // per), per * bi, bj)
    tiled = jnp.tile(groups, (1, 1, per))
    r = np.arange(per * bi)[:, None] // bi
    c = np.arange(per * bj)[None, :] // bj
    return jnp.where(jnp.asarray(r == c)[None], tiled, 0.0).reshape(depth, n // per, per * bi, per * bj)


def kernel(x_prompt, x_sample, p_prompt, p_sample, cache_k_a, cache_v_a, state_ret, state_conv, state_lru, w_in, rel_table, gn_gain, conv_w, conv_b, w_gate_a, b_gate_a, w_gate_x, b_gate_x, lru_lambda, w_branch, w_out, ln_gain, ln_bias, w_ple, w_ple_gate):
    depth = w_in.shape[0]
    bp, sp, d_model = x_prompt.shape
    bs, ss, _ = x_sample.shape
    assert cache_k_a.shape[2] == HIST and w_in.shape[2] == N_MIX_COLS + 3 * d_model
    assert N_MIX_COLS % d_model == 0
    tp = min(PROMPT_TILE, sp)
    assert sp % tp == 0 and tp % LANES == 0 and HIST % tp == 0 and ss % SUBLANES == 0
    alpha = float((2 * depth) ** 0.25)

    w_in_b = w_in.astype(BF16)
    row = lambda a: a.reshape(depth, 1, a.shape[-1])
    layer_w = (row(gn_gain), conv_w, row(conv_b), _block_diag_quads(w_gate_a).astype(BF16), row(b_gate_a),
               _block_diag_quads(w_gate_x).astype(BF16), row(b_gate_x), row(lru_lambda))
    merge_w = (w_branch.astype(BF16), w_out.astype(BF16), w_ple_gate.astype(BF16), w_ple.astype(BF16),
               row(ln_gain), row(ln_bias))

    bias_p, bias_s = _expand_bias(rel_table, tp, ss)
    tables_p = _tables(np.arange(sp), tp)
    tables_s = _tables(PAST_LEN + np.arange(ss), ss)
    cache = (cache_k_a.reshape(depth, bs, HIST, W_BRANCH), cache_v_a.reshape(depth, bs, HIST, W_BRANCH),
             state_ret, state_conv, state_lru.reshape(depth, bs, 1, W_BRANCH))
    pp = p_prompt.reshape(depth, bp * sp, -1)
    ps = p_sample.reshape(depth, bs * ss, -1)

    hp, hs = x_prompt, x_sample
    outs_p = outs_s = None
    for l in range(depth):
        y, *outs_p = _mixer_call(l, hp, tables_p, w_in_b, bias_p, layer_w, None, outs_p, T=tp, prompt=True)
        hp = _merge_call(l, hp.reshape(bp * sp, d_model), y.reshape(bp * sp, -1), pp, w_in_b, merge_w,
                         alpha=alpha).reshape(bp, sp, d_model)
        y, *outs_s = _mixer_call(l, hs, tables_s, w_in_b, bias_s, layer_w, cache, outs_s, T=ss, prompt=False)
        hs = _merge_call(l, hs.reshape(bs * ss, d_model), y.reshape(bs * ss, -1), ps, w_in_b, merge_w,
                         alpha=alpha).reshape(bs, ss, d_model)

    k_p, v_p, r_p, c_p, s_p = outs_p
    k_s, v_s, r_s, c_s, s_s = outs_s
    heads = lambda a: a.reshape(a.shape[:-1] + (N_HEADS, HEAD_DIM))
    return (hp, hs, heads(k_p), heads(v_p), heads(k_s), heads(v_s), r_p, r_s, c_p, c_s,
            s_p.reshape(depth, bp, W_BRANCH), s_s.reshape(depth, bs, W_BRANCH))
```

```python
import functools

import numpy as np

import jax
import jax.numpy as jnp
from jax import lax
from jax.experimental import pallas as pl
from jax.experimental.pallas import tpu as pltpu

F32 = jnp.float32
BF16 = jnp.bfloat16

CHUNK = 64
N_LEFT_CHUNKS = 8
HIST = N_LEFT_CHUNKS * CHUNK
HEAD_DIM = 64
N_HEADS = 8
W_BRANCH = N_HEADS * HEAD_DIM
N_PAIRS = N_HEADS // 2
LANES = 128
SUBLANES = 8
QUAD = 4 * HEAD_DIM
N_QUADS = N_HEADS // 4
REL_CLIP = 128
CONV_W = 4
LRU_C = 8.0
ROPE_BASE = 10000.0
LN_EPS = 1e-5
NEG_INF = -1e30
LOG2E = 1.4426950408889634
PAST_LEN = 1024
N_MIX_COLS = 10 * W_BRANCH
PROMPT_TILE = 256
MERGE_TILE = 1024
MERGE_ROWS = 256
SOFTMAX_ROWS = 16
SQRT_FLOOR = 1e-30
VMEM_LIMIT_BYTES = 56 * 1024 * 1024

_NT = (((1,), (1,)), ((), ()))
_TN = (((0,), (0,)), ((), ()))


def _const_spec(shape):
    zeros = (0,) * len(shape)
    return pl.BlockSpec(shape, lambda *_: zeros, pipeline_mode=pl.Buffered(1))


def _layer_spec(arr, layer, block=None, index=None):
    block = tuple(arr.shape[1:]) if block is None else tuple(block)
    index = (0,) * len(block) if index is None else tuple(index)
    return pl.BlockSpec((None,) + block, lambda *_: (layer,) + index, pipeline_mode=pl.Buffered(1))


def _sigmoid(x):
    return 0.5 * jnp.tanh(0.5 * x) + 0.5


def _silu(x):
    half = 0.5 * x
    return half * jnp.tanh(half) + half


def _bias_kernel(tab_ref, bp_ref, bs_ref, *, tq, ts):
    h = pl.program_id(0)

    blocks = {}

    def toeplitz_block(rows, i0, j0):
        i = lax.broadcasted_iota(jnp.int32, (rows, LANES), 0) + i0
        j = lax.broadcasted_iota(jnp.int32, (rows, LANES), 1) + j0
        if (rows, i0 - j0) in blocks:
            return blocks[rows, i0 - j0], i, j
        idx = jnp.clip(HIST + i - j, -REL_CLIP, REL_CLIP) + REL_CLIP
        d_lo = HIST + i0 - (j0 + LANES - 1)
        d_hi = HIST + i0 + rows - 1 - j0
        k_lo = min(max(d_lo, -REL_CLIP), REL_CLIP) + REL_CLIP
        k_hi = min(max(d_hi, -REL_CLIP), REL_CLIP) + REL_CLIP

        def body(k, acc):
            return jnp.where(idx == k, tab_ref[h, k], acc)

        blk = lax.fori_loop(k_lo, k_hi + 1, body, jnp.zeros((rows, LANES), F32), unroll=8) * LOG2E
        blocks[rows, i0 - j0] = blk
        return blk, i, j

    for ib in range(tq // LANES):
        for jb in range((HIST + tq) // LANES):
            blk, i, j = toeplitz_block(LANES, ib * LANES, jb * LANES)
            qc = i // CHUNK
            kc = j // CHUNK
            vis = (kc >= qc) & (kc <= qc + N_LEFT_CHUNKS)
            bp_ref[0, ib * LANES:(ib + 1) * LANES, jb * LANES:(jb + 1) * LANES] = jnp.where(vis, blk, NEG_INF)
    for jb in range(bs_ref.shape[2] // LANES):
        blk, _, _ = toeplitz_block(ts, 0, jb * LANES)
        bs_ref[0, :, jb * LANES:(jb + 1) * LANES] = blk


def _expand_bias(rel_table, tq, ts):
    depth, nh, nrel = rel_table.shape
    ws = -(-(HIST + ts) // LANES) * LANES
    bp, bs = pl.pallas_call(
        functools.partial(_bias_kernel, tq=tq, ts=ts),
        grid=(depth * nh,),
        in_specs=[pl.BlockSpec(memory_space=pltpu.SMEM)],
        out_specs=[pl.BlockSpec((1, tq, HIST + tq), lambda g: (g, 0, 0)),
                   pl.BlockSpec((1, ts, ws), lambda g: (g, 0, 0))],
        out_shape=[jax.ShapeDtypeStruct((depth * nh, tq, HIST + tq), F32),
                   jax.ShapeDtypeStruct((depth * nh, ts, ws), F32)],
        name="rel_bias",
    )(rel_table.reshape(depth * nh, nrel))
    return bp.reshape(depth, nh, tq, HIST + tq), bs.reshape(depth, nh, ts, ws)


def _mixer_kernel(*refs, T, prompt, n_tiles, n_alias):
    n_in = 19 if prompt else 24
    (x_ref, cos_ref, sin_ref, w_ref, bias_ref, dec_ref, xi_ref, zeta_ref, gmat_ref, bmask_ref, seg_ref, gn_ref,
     cw_ref, cb_ref, wga_ref, bga_ref, wgx_ref, bgx_ref, lam_ref) = refs[:19]
    if not prompt:
        kc_ref, vc_ref, s0_ref, c0_ref, h0_ref = refs[19:24]
    refs = refs[n_in + n_alias:]
    (y_ref, ko_ref, vo_ref, ret_ref, conv_ref, lru_ref,
     xb_s, qm_s, kbuf_s, vbuf_s, ya_s, yb_s, st_s, xbuf, hseq_s, h_s, sc_s, pr_s, rl_s) = refs

    t = pl.program_id(1)
    tail = CONV_W - 1
    lane = lax.broadcasted_iota(jnp.int32, (T, LANES), 1)
    first_head = lane < HEAD_DIM

    def split_heads(pair):
        zero = jnp.zeros_like(pair)
        return jnp.where(first_head, pair, zero), jnp.where(first_head, zero, pair)

    hist_tiles = HIST // T
    n_slots = hist_tiles + 1
    own_slot = lax.rem(t, n_slots) if prompt else 0
    own_rows = slice(0, T) if prompt else slice(HIST, HIST + T)

    if prompt:
        @pl.when(t == 0)
        def _():
            st_s[...] = jnp.zeros_like(st_s)
            xbuf[0:SUBLANES, :] = jnp.zeros((SUBLANES, W_BRANCH), F32)
            h_s[...] = jnp.zeros_like(h_s)
    else:
        kc = kc_ref[0].astype(BF16)
        vc = vc_ref[0].astype(BF16)
        for hp in range(N_PAIRS):
            sl = slice(hp * LANES, (hp + 1) * LANES)
            kbuf_s[0, hp, 0:HIST, :] = kc[:, sl]
            vbuf_s[0, hp, 0:HIST, :] = vc[:, sl]
        st_s[...] = jnp.zeros_like(st_s)
        for h in range(N_HEADS):
            d0 = (h % 4) * HEAD_DIM
            st_s[h // 4, d0:d0 + HEAD_DIM, d0:d0 + HEAD_DIM] = s0_ref[0, h]
        xbuf[0:SUBLANES, :] = jnp.zeros((SUBLANES, W_BRANCH), F32)
        xbuf[SUBLANES - tail:SUBLANES, :] = c0_ref[0]
        h_s[...] = h0_ref[0]

    xb_s[...] = x_ref[0].astype(BF16)

    def proj(c):
        cols = slice(c * W_BRANCH, (c + 1) * W_BRANCH)
        return jnp.dot(xb_s[...], w_ref[:, cols], preferred_element_type=F32)

    def quad_dot(a, w_of):
        return jnp.concatenate([jnp.dot(a[:, g * QUAD:(g + 1) * QUAD], w_of(g), preferred_element_type=F32)
                                for g in range(N_QUADS)], axis=1)

    tile_rows = pl.ds(pl.multiple_of(t * T, T), T)
    cos = cos_ref[tile_rows, :]
    sin = sin_ref[tile_rows, :]
    lane_w = lax.broadcasted_iota(jnp.int32, (T, W_BRANCH), 1)
    low_half = (lane_w % HEAD_DIM) < (HEAD_DIM // 2)

    def rope(v):
        partner = jnp.where(low_half, pltpu.roll(v, W_BRANCH - HEAD_DIM // 2, 1), pltpu.roll(v, HEAD_DIM // 2, 1))
        return v * cos + partner * sin

    qr = rope(proj(4))
    kr = rope(proj(5)) * (HEAD_DIM ** -0.5)
    q_in = qr.astype(BF16)
    q_x = (qr * xi_ref[...]).astype(BF16)
    k_in = kr.astype(BF16)
    k_z = (kr * zeta_ref[...]).astype(BF16)
    v_b = proj(6).astype(BF16)
    gated_b = _silu(proj(7))
    for g in range(N_QUADS):
        gs = slice(g * QUAD, (g + 1) * QUAD)
        state = st_s[g]
        yb_s[:, gs] = jnp.dot(q_x[:, gs], state.astype(BF16), preferred_element_type=F32)
        outer = lax.dot_general(k_z[:, gs], v_b[:, gs], _TN, preferred_element_type=F32)
        st_s[g] = gmat_ref[g] * state + bmask_ref[...] * outer
    for hp in range(N_PAIRS):
        sl = slice(hp * LANES, (hp + 1) * LANES)
        k_heads = split_heads(k_in[:, sl])
        v_heads = split_heads(v_b[:, sl])
        acc = yb_s[:, sl]
        for e in range(2):
            h = 2 * hp + e
            qk = lax.dot_general(q_in[:, sl], k_heads[e], _NT, preferred_element_type=F32)
            inner = (qk * dec_ref[h]).astype(BF16)
            acc = acc + jnp.dot(inner, v_heads[e], preferred_element_type=F32)
        yb_s[:, sl] = acc

    def seg_mean(a):
        return quad_dot(a.astype(BF16), lambda g: seg_ref[...])

    yb = yb_s[...]
    dev = yb - seg_mean(yb)
    var = seg_mean(dev * dev)
    ybn = dev * lax.rsqrt(var + LN_EPS) * gn_ref[...]
    y_ref[0, :, W_BRANCH:2 * W_BRANCH] = (ybn * gated_b).astype(BF16)

    def own_keys(c, out_ref, buf_s):
        full = proj(c)
        out_ref[0] = full
        half = full.astype(BF16)
        for hp in range(N_PAIRS):
            buf_s[own_slot, hp, own_rows, :] = half[:, hp * LANES:(hp + 1) * LANES]

    xr = proj(8)
    xbuf[SUBLANES:SUBLANES + T, :] = xr
    xc = cb_ref[...] + xr * cw_ref[tail:CONV_W, :]
    for j in range(tail):
        xc = xc + xbuf[SUBLANES - tail + j:SUBLANES - tail + j + T, :] * cw_ref[j:j + 1, :]
    xcb = xc.astype(BF16)
    gate_a = quad_dot(xcb, lambda g: wga_ref[g])
    gate_x = quad_dot(xcb, lambda g: wgx_ref[g])
    qa = (proj(0) * (HEAD_DIM ** -0.5 * LOG2E)).astype(BF16)
    for hp in range(N_PAIRS):
        qm_s[2 * hp], qm_s[2 * hp + 1] = split_heads(qa[:, hp * LANES:(hp + 1) * LANES])
    r = _sigmoid(gate_a + bga_ref[...])
    ig = _sigmoid(gate_x + bgx_ref[...])
    log_a = (-LRU_C) * r * jax.nn.softplus(-lam_ref[...])
    a_t = jnp.exp(log_a)
    gap = 1.0 - a_t * a_t
    b_t = gap * lax.rsqrt(jnp.maximum(gap, SQRT_FLOOR)) * ig * xc
    n_groups = T // SUBLANES
    a3 = a_t.reshape(n_groups, SUBLANES, W_BRANCH)
    b3 = b_t.reshape(n_groups, SUBLANES, W_BRANCH)
    sub = lax.broadcasted_iota(jnp.int32, (n_groups, SUBLANES, W_BRANCH), 1)
    step = 1
    while step < SUBLANES:
        if step == 2:
            own_keys(1, ko_ref, kbuf_s)
        keep = sub >= step
        a_prev = jnp.where(keep, pltpu.roll(a3, step, 1), 1.0)
        b_prev = jnp.where(keep, pltpu.roll(b3, step, 1), 0.0)
        b3 = a3 * b_prev + b3
        a3 = a3 * a_prev
        step *= 2
    carry = h_s[...]
    for g in range(n_groups):
        hg = a3[g] * carry + b3[g]
        hseq_s[g * SUBLANES:(g + 1) * SUBLANES, :] = hg
        carry = hg[SUBLANES - 1:SUBLANES, :]
    h_s[...] = carry
    own_keys(2, vo_ref, vbuf_s)
    gated_c = _silu(proj(9))
    y_ref[0, :, 2 * W_BRANCH:3 * W_BRANCH] = (hseq_s[...] * gated_c).astype(BF16)
    xbuf[0:SUBLANES, :] = xbuf[T:T + SUBLANES, :]


    def attend(hist_tiles_present):
        if prompt:
            blocks = [(lax.rem(t + n_slots - back, n_slots), slice(0, T), slice(HIST - back * T, HIST - (back - 1) * T))
                      for back in range(hist_tiles_present, 0, -1)]
        else:
            blocks = [(0, slice(0, HIST), slice(0, HIST))]
        blocks.append((own_slot, own_rows, slice(HIST, HIST + T)))
        c0 = blocks[0][2].start

        def scores(h):
            for slot, rows, cols in blocks:
                sc_s[h % 2, :, cols] = (lax.dot_general(qm_s[h], kbuf_s[slot, h // 2, rows, :], _NT,
                                                        preferred_element_type=F32) + bias_ref[h, :, cols])

        def softmax(h):
            for r0 in range(0, T, SOFTMAX_ROWS):
                rows = slice(r0, r0 + SOFTMAX_ROWS)
                s = sc_s[h % 2, rows, c0:HIST + T]
                p = jnp.exp2(s - jnp.max(s, axis=1, keepdims=True))
                pr_s[h % 2, rows, c0:HIST + T] = p.astype(BF16)
                rl_s[h % 2, rows, :] = jnp.broadcast_to(1.0 / jnp.sum(p, axis=1, keepdims=True),
                                                        (SOFTMAX_ROWS, LANES))

        def weighted_values(h):
            sl = slice((h // 2) * LANES, (h // 2 + 1) * LANES)
            o = None
            for slot, rows, cols in blocks:
                part = jnp.dot(pr_s[h % 2, :, cols], vbuf_s[slot, h // 2, rows, :], preferred_element_type=F32)
                o = part if o is None else o + part
            o = o * rl_s[h % 2]
            ya_s[:, sl] = o if h % 2 == 0 else jnp.where(first_head, ya_s[:, sl], o)

        scores(0)
        for h in range(N_HEADS):
            if h + 1 < N_HEADS:
                scores(h + 1)
            else:
                gated_a = _silu(proj(3))
            softmax(h)
            weighted_values(h)
        y_ref[0, :, 0:W_BRANCH] = (ya_s[...] * gated_a).astype(BF16)

    if prompt:
        for k in range(hist_tiles):
            pl.when(t == k)(functools.partial(attend, k))
        pl.when(t >= hist_tiles)(functools.partial(attend, hist_tiles))
    else:
        attend(None)

    def write_state():
        for h in range(N_HEADS):
            d0 = (h % 4) * HEAD_DIM
            ret_ref[0, h] = st_s[h // 4, d0:d0 + HEAD_DIM, d0:d0 + HEAD_DIM]
        conv_ref[0] = xbuf[SUBLANES - tail:SUBLANES, :]
        lru_ref[0] = h_s[...]

    if prompt:
        pl.when(t == n_tiles - 1)(write_state)
    else:
        write_state()


def _mixer_call(layer, x, tables, w_in_b, bias, layer_w, cache, prev, *, T, prompt):
    B, S, D = x.shape
    depth = w_in_b.shape[0]
    n_tiles = S // T
    kv_rows = min(HIST, S)
    kv_first = n_tiles - kv_rows // T
    cos, sin, dec, xi, zeta, gmat, bmask, seg = tables
    gn, cw, cb, wga, bga, wgx, bgx, lam = layer_w

    in_specs = [
        pl.BlockSpec((1, T, D), lambda b, t: (b, t, 0)),
        _const_spec(cos.shape), _const_spec(sin.shape),
        _layer_spec(w_in_b, layer, block=(D, N_MIX_COLS)),
        _layer_spec(bias, layer),
        _const_spec(dec.shape), _const_spec(xi.shape), _const_spec(zeta.shape),
        _const_spec(gmat.shape), _const_spec(bmask.shape), _const_spec(seg.shape),
    ] + [_layer_spec(a, layer) for a in layer_w]
    args = [x, cos, sin, w_in_b, bias, dec, xi, zeta, gmat, bmask, seg, gn, cw, cb, wga, bga, wgx, bgx, lam]
    if not prompt:
        for a in cache:
            blk = (1,) + tuple(a.shape[2:])
            zeros = (0,) * (len(blk) - 1)
            in_specs.append(pl.BlockSpec((None,) + blk, lambda b, t, zeros=zeros: (layer, b) + zeros))
        args += list(cache)
    aliases = {}
    if prev is not None:
        for k, a in enumerate(prev):
            in_specs.append(pl.BlockSpec(memory_space=pl.ANY))
            aliases[len(args)] = k + 1
            args.append(a)

    def kv_map(b, t):
        return (layer, b, jnp.maximum(t - kv_first, 0), 0)

    out_specs = [
        pl.BlockSpec((1, T, 3 * W_BRANCH), lambda b, t: (b, t, 0)),
        pl.BlockSpec((None, 1, T, W_BRANCH), kv_map),
        pl.BlockSpec((None, 1, T, W_BRANCH), kv_map),
        pl.BlockSpec((None, 1, N_HEADS, HEAD_DIM, HEAD_DIM), lambda b, t: (layer, b, 0, 0, 0)),
        pl.BlockSpec((None, 1, CONV_W - 1, W_BRANCH), lambda b, t: (layer, b, 0, 0)),
        pl.BlockSpec((None, 1, 1, W_BRANCH), lambda b, t: (layer, b, 0, 0)),
    ]
    out_shape = [
        jax.ShapeDtypeStruct((B, S, 3 * W_BRANCH), BF16),
        jax.ShapeDtypeStruct((depth, B, kv_rows, W_BRANCH), F32),
        jax.ShapeDtypeStruct((depth, B, kv_rows, W_BRANCH), F32),
        jax.ShapeDtypeStruct((depth, B, N_HEADS, HEAD_DIM, HEAD_DIM), F32),
        jax.ShapeDtypeStruct((depth, B, CONV_W - 1, W_BRANCH), F32),
        jax.ShapeDtypeStruct((depth, B, 1, W_BRANCH), F32),
    ]
    kv_buf_shape = (HIST // T + 1, N_PAIRS, T, LANES) if prompt else (1, N_PAIRS, HIST + T, LANES)
    scratch = [
        pltpu.VMEM((T, D), BF16),
        pltpu.VMEM((N_HEADS, T, LANES), BF16),
        pltpu.VMEM(kv_buf_shape, BF16),
        pltpu.VMEM(kv_buf_shape, BF16),
        pltpu.VMEM((T, W_BRANCH), F32),
        pltpu.VMEM((T, W_BRANCH), F32),
        pltpu.VMEM((N_QUADS, QUAD, QUAD), F32),
        pltpu.VMEM((T + SUBLANES, W_BRANCH), F32),
        pltpu.VMEM((T, W_BRANCH), F32),
        pltpu.VMEM((1, W_BRANCH), F32),
        pltpu.VMEM((2, T, HIST + T), F32),
        pltpu.VMEM((2, T, HIST + T), BF16),
        pltpu.VMEM((2, T, LANES), F32),
    ]
    return pl.pallas_call(
        functools.partial(_mixer_kernel, T=T, prompt=prompt, n_tiles=n_tiles, n_alias=len(aliases)),
        grid=(B, n_tiles),
        in_specs=in_specs,
        out_specs=out_specs,
        out_shape=out_shape,
        scratch_shapes=scratch,
        input_output_aliases=aliases,
        compiler_params=pltpu.CompilerParams(
            dimension_semantics=("arbitrary", "arbitrary"), vmem_limit_bytes=VMEM_LIMIT_BYTES,
            ),
        name="mixer_prompt" if prompt else "mixer_sample",
    )(*args)


def _merge_kernel(x_ref, y_ref, p_ref, wg0_ref, wg1_ref, wg2_ref, wb_ref, wo_ref, wpg_ref, wpe_ref, lng_ref,
                  lnb_ref, o_ref, m_s, r_s, *, alpha, rows_per_pass):
    passes = [slice(r0, r0 + rows_per_pass) for r0 in range(0, x_ref.shape[0], rows_per_pass)]

    def branch_merge(rows):
        xb = x_ref[rows, :].astype(BF16)
        merged = None
        for b, wg_ref in enumerate((wg0_ref, wg1_ref, wg2_ref)):
            gate = _sigmoid(jnp.dot(xb, wg_ref[...], preferred_element_type=F32))
            term = gate * jnp.dot(y_ref[rows, b * W_BRANCH:(b + 1) * W_BRANCH], wb_ref[b],
                                  preferred_element_type=F32)
            merged = term if merged is None else merged + term
        m_s[rows, :] = merged.astype(BF16)

    def residual(rows):
        r = alpha * x_ref[rows, :] + jnp.dot(m_s[rows, :], wo_ref[...], preferred_element_type=F32)
        pgate = _sigmoid(jnp.dot(r.astype(BF16), wpg_ref[...], preferred_element_type=F32))
        r_s[rows, :] = r + pgate * jnp.dot(p_ref[rows, :].astype(BF16), wpe_ref[...], preferred_element_type=F32)

    def layer_norm(rows):
        r = r_s[rows, :]
        mu = jnp.mean(r, axis=-1, keepdims=True)
        dev = r - mu
        var = jnp.mean(dev * dev, axis=-1, keepdims=True)
        o_ref[rows, :] = dev * lax.rsqrt(var + LN_EPS) * lng_ref[...] + lnb_ref[...]

    branch_merge(passes[0])
    residual(passes[0])
    for prev, rows in zip(passes[:-1], passes[1:]):
        branch_merge(rows)
        layer_norm(prev)
        residual(rows)
    layer_norm(passes[-1])


def _merge_call(layer, x2, y2, p3, w_in_b, merge_w, *, alpha):
    n, d = x2.shape
    tm = min(MERGE_TILE, n)
    rows_per_pass = min(MERGE_ROWS, tm)
    gate_block0 = N_MIX_COLS // d
    return pl.pallas_call(
        functools.partial(_merge_kernel, alpha=alpha, rows_per_pass=rows_per_pass),
        grid=(n // tm,),
        scratch_shapes=[pltpu.VMEM((tm, d), BF16),
                        pltpu.VMEM((tm, d), F32)],
        in_specs=[
            pl.BlockSpec((tm, d), lambda i: (i, 0)),
            pl.BlockSpec((tm, y2.shape[1]), lambda i: (i, 0)),
            pl.BlockSpec((None, tm, p3.shape[2]), lambda i: (layer, i, 0)),
        ] + [_layer_spec(w_in_b, layer, block=(d, d), index=(0, gate_block0 + b)) for b in range(3)]
          + [_layer_spec(a, layer) for a in merge_w],
        out_specs=pl.BlockSpec((tm, d), lambda i: (i, 0)),
        out_shape=jax.ShapeDtypeStruct((n, d), F32),
        compiler_params=pltpu.CompilerParams(
            dimension_semantics=("arbitrary",), vmem_limit_bytes=VMEM_LIMIT_BYTES),
        name="merge",
    )(x2, y2, p3, w_in_b, w_in_b, w_in_b, *merge_w)


def _tables(pos, blk):
    half = HEAD_DIM // 2
    inv = ROPE_BASE ** (-np.arange(half, dtype=np.float64) / half)
    ang = np.asarray(pos, np.float64)[:, None] * inv[None, :]
    c, s = np.cos(ang), np.sin(ang)
    cos = np.tile(np.concatenate([c, c], axis=-1), (1, N_HEADS))
    sin = np.tile(np.concatenate([-s, s], axis=-1), (1, N_HEADS))

    log_g = np.log1p(-np.exp2(-5.0 - np.arange(N_HEADS, dtype=np.float64)))
    i = np.arange(blk, dtype=np.float64)
    diff = i[:, None] - i[None, :]
    decay = np.where(diff >= 0, np.exp(log_g[:, None, None] * np.maximum(diff, 0.0)), 0.0)
    zeta = np.exp(log_g[:, None] * (blk - 1.0 - i)[None, :])
    xi = np.exp(log_g[:, None] * (i + 1.0)[None, :])
    g_blk = np.exp(log_g * blk)
    xi_full = np.repeat(xi.T, HEAD_DIM, axis=1)
    zeta_full = np.repeat(zeta.T, HEAD_DIM, axis=1)
    q = np.arange(QUAD)
    bmask = (q[:, None] // HEAD_DIM == q[None, :] // HEAD_DIM).astype(np.float64)
    gmat = np.stack([bmask * np.repeat(g_blk[4 * g:4 * g + 4], HEAD_DIM)[None, :] for g in range(N_QUADS)])
    seg = bmask / HEAD_DIM
    f32 = lambda a: jnp.asarray(a, F32)
    return (f32(cos), f32(sin), f32(decay), f32(xi_full), f32(zeta_full), f32(gmat), f32(bmask),
            jnp.asarray(seg, BF16))


def _block_diag_quads(w):
    depth, n, bi, bj = w.shape
    per = QUAD // bi
    groups = w.reshape(depth * (n // per), per * bi, bj)
    tiled = jnp.tile(groups, (1, 1, per))
    r = np.arange(per * bi)[:, None] // bi
    c = np.arange(per * bj)[None, :] // bj
    return jnp.where(jnp.asarray(r == c)[None], tiled, 0.0).reshape(depth, n // per, per * bi, per * bj)


def kernel(x_prompt, x_sample, p_prompt, p_sample, cache_k_a, cache_v_a, state_ret, state_conv, state_lru, w_in, rel_table, gn_gain, conv_w, conv_b, w_gate_a, b_gate_a, w_gate_x, b_gate_x, lru_lambda, w_branch, w_out, ln_gain, ln_bias, w_ple, w_ple_gate):
    depth = w_in.shape[0]
    bp, sp, d_model = x_prompt.shape
    bs, ss, _ = x_sample.shape
    assert cache_k_a.shape[2] == HIST and w_in.shape[2] == N_MIX_COLS + 3 * d_model
    assert N_MIX_COLS % d_model == 0
    tp = min(PROMPT_TILE, sp)
    assert sp % tp == 0 and tp % LANES == 0 and HIST % tp == 0 and ss % SUBLANES == 0
    alpha = float((2 * depth) ** 0.25)

    w_in_b = w_in.astype(BF16)
    row = lambda a: a.reshape(depth, 1, a.shape[-1])
    layer_w = (row(gn_gain), conv_w, row(conv_b), _block_diag_quads(w_gate_a).astype(BF16), row(b_gate_a),
               _block_diag_quads(w_gate_x).astype(BF16), row(b_gate_x), row(lru_lambda))
    merge_w = (w_branch.astype(BF16), w_out.astype(BF16), w_ple_gate.astype(BF16), w_ple.astype(BF16),
               row(ln_gain), row(ln_bias))

    bias_p, bias_s = _expand_bias(rel_table, tp, ss)
    tables_p = _tables(np.arange(sp), tp)
    tables_s = _tables(PAST_LEN + np.arange(ss), ss)
    cache = (cache_k_a.reshape(depth, bs, HIST, W_BRANCH), cache_v_a.reshape(depth, bs, HIST, W_BRANCH),
             state_ret, state_conv, state_lru.reshape(depth, bs, 1, W_BRANCH))
    pp = p_prompt.reshape(depth, bp * sp, -1)
    ps = p_sample.reshape(depth, bs * ss, -1)

    hp, hs = x_prompt, x_sample
    outs_p = outs_s = None
    for l in range(depth):
        y, *outs_p = _mixer_call(l, hp, tables_p, w_in_b, bias_p, layer_w, None, outs_p, T=tp, prompt=True)
        hp = _merge_call(l, hp.reshape(bp * sp, d_model), y.reshape(bp * sp, -1), pp, w_in_b, merge_w,
                         alpha=alpha).reshape(bp, sp, d_model)
        y, *outs_s = _mixer_call(l, hs, tables_s, w_in_b, bias_s, layer_w, cache, outs_s, T=ss, prompt=False)
        hs = _merge_call(l, hs.reshape(bs * ss, d_model), y.reshape(bs * ss, -1), ps, w_in_b, merge_w,
                         alpha=alpha).reshape(bs, ss, d_model)

    k_p, v_p, r_p, c_p, s_p = outs_p
    k_s, v_s, r_s, c_s, s_s = outs_s
    heads = lambda a: a.reshape(a.shape[:-1] + (N_HEADS, HEAD_DIM))
    return (hp, hs, heads(k_p), heads(v_p), heads(k_s), heads(v_s), r_p, r_s, c_p, c_s,
            s_p.reshape(depth, bp, W_BRANCH), s_s.reshape(depth, bs, W_BRANCH))
```
